```python
import jax, jax.numpy as jnp
from jax import lax
import numpy as np

D_MODEL = 1024
BATCH = 8
SEQ = 2048
DEPTH = 4

CHUNK = 64
N_LEFT_CHUNKS = 8
HEAD_DIM = 64
D_ATT = 3 * D_MODEL // 8
D_RET = 3 * D_MODEL // 8
D_CONV = D_MODEL - D_ATT - D_RET
D_MIX = D_ATT + D_CONV + D_RET
H_ATT = D_ATT // HEAD_DIM
H_RET = D_RET // HEAD_DIM
CONV_K = 31
REL_CLIP = 128
D_FF = 4 * D_MODEL
D_PLE = 256
ROPE_THETA = 10000.0
EPS = 1e-6
SPLIT_SIZES = [D_ATT, D_ATT, D_ATT, 2 * D_CONV, D_RET, D_RET, D_RET, D_RET]
D_IN = sum(SPLIT_SIZES)

kernel_name = "hymba_style_chunk_causal_hybrid_trunk"


def rmsnorm(x, g):
    xf = x.astype(jnp.float32)
    y = xf * lax.rsqrt(jnp.mean(xf * xf, axis=-1, keepdims=True) + EPS)
    return (y * g.astype(jnp.float32)).astype(x.dtype)


def layernorm(x, g, b):
    xf = x.astype(jnp.float32)
    mu = jnp.mean(xf, axis=-1, keepdims=True)
    var = jnp.mean(jnp.square(xf - mu), axis=-1, keepdims=True)
    y = (xf - mu) * lax.rsqrt(var + EPS)
    return (y * g.astype(jnp.float32) + b.astype(jnp.float32)).astype(x.dtype)


def to_chunk_heads(t, n_heads):
    b, s, _ = t.shape
    return t.reshape(b, s // CHUNK, CHUNK, n_heads, HEAD_DIM).transpose(0, 3, 1, 2, 4)


def from_chunk_heads(t):
    b, h, nc, c, d = t.shape
    return t.transpose(0, 2, 3, 1, 4).reshape(b, nc * c, h * d)


def chunked_attention(q, k, v, qn_g, kn_g, rel_bias):
    q = rmsnorm(to_chunk_heads(q, H_ATT), qn_g)
    k = rmsnorm(to_chunk_heads(k, H_ATT), kn_g)
    v = to_chunk_heads(v, H_ATT)
    b, h, nc, c, d = q.shape
    band = N_LEFT_CHUNKS + 1
    pad = ((0, 0), (0, 0), (N_LEFT_CHUNKS, 0), (0, 0), (0, 0))
    idx = jnp.arange(nc)[:, None] + jnp.arange(band)[None, :]
    kb = jnp.pad(k, pad)[:, :, idx].reshape(b, h, nc, band * c, d)
    vb = jnp.pad(v, pad)[:, :, idx].reshape(b, h, nc, band * c, d)
    valid = jnp.repeat(idx >= N_LEFT_CHUNKS, c, axis=1)
    q_pos = N_LEFT_CHUNKS * c + jnp.arange(c)
    rel = q_pos[:, None] - jnp.arange(band * c)[None, :]
    bias = rel_bias.astype(jnp.float32)[:, jnp.clip(rel, -REL_CLIP, REL_CLIP) + REL_CLIP]
    scores = jnp.einsum('bhncd,bhnkd->bhnck', q, kb).astype(jnp.float32) * (d ** -0.5)
    scores = scores + bias[None, :, None]
    scores = jnp.where(valid[None, None, :, None, :], scores, -1e30)
    probs = jax.nn.softmax(scores, axis=-1).astype(v.dtype)
    out = jnp.einsum('bhnck,bhnkd->bhncd', probs, vb)
    return from_chunk_heads(out)


def conformer_conv(u, conv_w, conv_b, ln_g, ln_b, pw_w, pw_b):
    a, gate = jnp.split(u, 2, axis=-1)
    glu = a * jax.nn.sigmoid(gate)
    y = lax.conv_general_dilated(
        glu, conv_w[:, None, :].astype(glu.dtype), window_strides=(1,),
        padding=[(CONV_K - 1, 0)], dimension_numbers=('NWC', 'WIO', 'NWC'),
        feature_group_count=D_CONV)
    y = layernorm(y + conv_b, ln_g, ln_b)
    y = jax.nn.silu(y)
    return y @ pw_w + pw_b


def rope(t, cos, sin):
    t1, t2 = jnp.split(t, 2, axis=-1)
    c = cos[None, :, None, :]
    s = sin[None, :, None, :]
    return jnp.concatenate([t1 * c - t2 * s, t1 * s + t2 * c], axis=-1)


def retention(q, k, v, g, gn_g):
    b, s, _ = q.shape
    nc = s // CHUNK
    pos = jnp.arange(s, dtype=jnp.float32)
    inv_freq = ROPE_THETA ** (-jnp.arange(0, HEAD_DIM, 2, dtype=jnp.float32) / HEAD_DIM)
    ang = pos[:, None] * inv_freq[None, :]
    cos, sin = jnp.cos(ang), jnp.sin(ang)
    qf = rope(q.astype(jnp.float32).reshape(b, s, H_RET, HEAD_DIM), cos, sin) * (HEAD_DIM ** -0.5)
    kf = rope(k.astype(jnp.float32).reshape(b, s, H_RET, HEAD_DIM), cos, sin)
    qc = qf.reshape(b, nc, CHUNK, H_RET, HEAD_DIM).transpose(0, 3, 1, 2, 4)
    kc = kf.reshape(b, nc, CHUNK, H_RET, HEAD_DIM).transpose(0, 3, 1, 2, 4)
    vc = to_chunk_heads(v.astype(jnp.float32), H_RET)
    log_gamma = jnp.log(1.0 - 2.0 ** (-5.0 - jnp.arange(H_RET, dtype=jnp.float32)))
    n = jnp.arange(CHUNK, dtype=jnp.float32)
    dist = jnp.abs(n[:, None] - n[None, :])
    d_intra = jnp.exp(log_gamma[:, None, None] * dist[None])
    intra = jnp.einsum('bhncm,bhnme->bhnce',
                       jnp.einsum('bhncd,bhnmd->bhncm', qc, kc) * d_intra[None, :, None], vc)
    k_dec = kc * jnp.exp(log_gamma[:, None] * (CHUNK - 1 - n)[None, :])[None, :, None, :, None]
    kv = jnp.einsum('bhnmd,bhnme->bhnde', k_dec, vc)
    chunk_decay = jnp.exp(log_gamma * CHUNK)[None, :, None, None]

    def step(state, kv_c):
        return chunk_decay * state + kv_c, state

    init = jnp.zeros((b, H_RET, HEAD_DIM, HEAD_DIM), jnp.float32)
    _, prev = lax.scan(step, init, jnp.moveaxis(kv, 2, 0))
    prev = jnp.moveaxis(prev, 0, 2)
    q_dec = qc * jnp.exp(log_gamma[:, None] * (n + 1.0)[None, :])[None, :, None, :, None]
    cross = jnp.einsum('bhncd,bhnde->bhnce', q_dec, prev)
    out = intra + cross
    mu = jnp.mean(out, axis=-1, keepdims=True)
    var = jnp.mean(jnp.square(out - mu), axis=-1, keepdims=True)
    out = from_chunk_heads((out - mu) * lax.rsqrt(var + EPS)) * gn_g.astype(jnp.float32)
    return (jax.nn.silu(g.astype(jnp.float32)) * out).astype(q.dtype)


def setup_inputs(seed: int = 0) -> dict:
    key = jax.random.key(seed)
    ks = jax.random.split(key, 24)
    f32 = jnp.float32

    def nrm(k, shape, scale):
        return jax.random.normal(k, shape, f32) * scale

    def gain(k, shape):
        return 1.0 + 0.02 * jax.random.normal(k, shape, f32)

    L = DEPTH
    return {
        "x": nrm(ks[0], (BATCH, SEQ, D_MODEL), 1.0),
        "p": nrm(ks[1], (DEPTH, BATCH, SEQ, D_PLE), 1.0),
        "norm_mix_g": gain(ks[2], (L, D_MODEL)),
        "w_in": nrm(ks[3], (L, D_MODEL, D_IN), D_MODEL ** -0.5),
        "qn_g": gain(ks[4], (L, HEAD_DIM)),
        "kn_g": gain(ks[5], (L, HEAD_DIM)),
        "rel_bias": nrm(ks[6], (L, H_ATT, 2 * REL_CLIP + 1), 0.1),
        "conv_w": nrm(ks[7], (L, CONV_K, D_CONV), CONV_K ** -0.5),
        "conv_b": nrm(ks[8], (L, D_CONV), 0.02),
        "conv_ln_g": gain(ks[9], (L, D_CONV)),
        "conv_ln_b": nrm(ks[10], (L, D_CONV), 0.02),
        "conv_pw_w": nrm(ks[11], (L, D_CONV, D_CONV), D_CONV ** -0.5),
        "conv_pw_b": nrm(ks[12], (L, D_CONV), 0.02),
        "ret_gn_g": gain(ks[13], (L, D_RET)),
        "w_o": nrm(ks[14], (L, D_MIX, D_MODEL), D_MIX ** -0.5),
        "norm_ffn_g": gain(ks[15], (L, D_MODEL)),
        "w1": nrm(ks[16], (L, D_MODEL, D_FF), D_MODEL ** -0.5),
        "w2": nrm(ks[17], (L, D_FF, D_MODEL), D_FF ** -0.5),
        "norm_ple_g": gain(ks[18], (L, D_MODEL)),
        "w_pg": nrm(ks[19], (L, D_MODEL, D_MODEL), D_MODEL ** -0.5),
        "w_ple": nrm(ks[20], (L, D_PLE, D_MODEL), D_PLE ** -0.5),
    }


def reference(x, p, norm_mix_g, w_in, qn_g, kn_g, rel_bias, conv_w, conv_b, conv_ln_g,
              conv_ln_b, conv_pw_w, conv_pw_b, ret_gn_g, w_o, norm_ffn_g, w1, w2,
              norm_ple_g, w_pg, w_ple):
    offsets = np.cumsum(SPLIT_SIZES)[:-1].tolist()
    h = x
    for i in range(DEPTH):
        xn = rmsnorm(h, norm_mix_g[i])
        proj = xn @ w_in[i]
        qa, ka, va, uc, qr, kr, vr, gr = jnp.split(proj, offsets, axis=-1)
        att = chunked_attention(qa, ka, va, qn_g[i], kn_g[i], rel_bias[i])
        conv = conformer_conv(uc, conv_w[i], conv_b[i], conv_ln_g[i], conv_ln_b[i],
                              conv_pw_w[i], conv_pw_b[i])
        ret = retention(qr, kr, vr, gr, ret_gn_g[i])
        h = h + jnp.concatenate([att, conv, ret], axis=-1) @ w_o[i]
        hn = rmsnorm(h, norm_ffn_g[i])
        h = h + jnp.square(jax.nn.relu(hn @ w1[i])) @ w2[i]
        gate = jax.nn.sigmoid(rmsnorm(h, norm_ple_g[i]) @ w_pg[i])
        h = h + gate * (p[i] @ w_ple[i])
    return h
```

```python
from functools import partial

import jax
import jax.numpy as jnp
from jax import lax
from jax.experimental import pallas as pl
from jax.experimental.pallas import tpu as pltpu

D_MODEL = 1024
CHUNK = 64
N_LEFT_CHUNKS = 8
HEAD_DIM = 64
D_ATT = 384
D_RET = 384
D_CONV = 256
H_ATT = D_ATT // HEAD_DIM
H_RET = D_RET // HEAD_DIM
CONV_K = 31
REL_CLIP = 128
D_FF = 4 * D_MODEL
D_PLE = 256
ROPE_THETA = 10000.0
EPS = 1e-6
D_IN = 3 * D_ATT + 2 * D_CONV + 4 * D_RET

OFF_QA, OFF_KA, OFF_VA = 0, D_ATT, 2 * D_ATT
OFF_CA = 3 * D_ATT
OFF_CG = OFF_CA + D_CONV
OFF_QR = OFF_CA + 2 * D_CONV
OFF_KR = OFF_QR + D_RET
OFF_VR = OFF_KR + D_RET
OFF_GR = OFF_VR + D_RET
MIX_ATT, MIX_CONV, MIX_RET = 0, D_ATT, D_ATT + D_CONV

LANES = 128
N_PAIRS = D_ATT // LANES
SEQ_TILE = N_LEFT_CHUNKS * CHUNK
Q_BLOCK = 2 * CHUNK
BAND = (N_LEFT_CHUNKS + 2) * CHUNK
CONV_HIST = 32
FFN_TILE = 512
FF_BLOCK = 512
NEG = -1e30
VMEM_LIMIT_BYTES = 56 * 1024 * 1024

_NT = (((1,), (1,)), ((), ()))
_TN = (((0,), (0,)), ((), ()))


def _rms(x, g):
    return (x * lax.rsqrt(jnp.mean(x * x, axis=-1, keepdims=True) + EPS)) * g


def _dot(a, b):
    return jnp.dot(a, b, preferred_element_type=jnp.float32)


def _pair_rms(x, low, g):
    x2 = x * x
    s0 = jnp.sum(jnp.where(low, x2, 0.0), axis=-1, keepdims=True)
    s1 = jnp.sum(jnp.where(low, 0.0, x2), axis=-1, keepdims=True)
    r0 = lax.rsqrt(s0 * (1.0 / HEAD_DIM) + EPS)
    r1 = lax.rsqrt(s1 * (1.0 / HEAD_DIM) + EPS)
    return (x * jnp.where(low, r0, r1)) * g


def _swap_halves(x, first_half):
    return jnp.where(first_half, pltpu.roll(x, 96, 1), pltpu.roll(x, 32, 1))


def _mixer_kernel(h_ref, g_ref, w_in_ref, qn_ref, kn_ref, bias_ref,
                  cw_ref, cb_ref, lng_ref, lnb_ref, pww_ref, pwb_ref,
                  cos_ref, sin_ref, dmat_ref, qdec_ref, kdec_ref, cdec_ref, gn_ref,
                  w_o_ref, o_ref,
                  xn_s, proj_s, k_s, v_s, glu_s, st_s, mix_s):
    t = pl.program_id(1)
    ts = SEQ_TILE

    @pl.when(t == 0)
    def _():
        k_s[0:ts, :] = jnp.zeros((ts, D_ATT), jnp.bfloat16)
        v_s[0:ts, :] = jnp.zeros((ts, D_ATT), jnp.bfloat16)
        glu_s[0:CONV_HIST, :] = jnp.zeros((CONV_HIST, D_CONV), jnp.float32)
        st_s[...] = jnp.zeros_like(st_s)

    xn_s[...] = _rms(h_ref[0], g_ref[...]).astype(jnp.bfloat16)
    for c0 in range(0, D_IN, 640):
        proj_s[:, c0:c0 + 640] = _dot(xn_s[...], w_in_ref[:, c0:c0 + 640])

    lane = lax.broadcasted_iota(jnp.int32, (1, LANES), 1)
    low = lane < HEAD_DIM
    first_half = (lane & 32) == 0

    for p in range(N_PAIRS):
        sl = slice(p * LANES, (p + 1) * LANES)
        q = proj_s[:, OFF_QA + p * LANES:OFF_QA + (p + 1) * LANES]
        k = proj_s[:, OFF_KA + p * LANES:OFF_KA + (p + 1) * LANES]
        qn = _pair_rms(q, low, qn_ref[...]) * (HEAD_DIM ** -0.5)
        proj_s[:, OFF_QA + p * LANES:OFF_QA + (p + 1) * LANES] = qn
        k_s[ts:2 * ts, sl] = _pair_rms(k, low, kn_ref[...]).astype(jnp.bfloat16)
    v_s[ts:2 * ts, :] = proj_s[:, OFF_VA:OFF_VA + D_ATT].astype(jnp.bfloat16)

    col = lax.broadcasted_iota(jnp.int32, (1, BAND), 1)
    for c2 in range(ts // Q_BLOCK):
        r0 = c2 * Q_BLOCK
        first_valid = jnp.where(t == 0, ts - r0, 0)
        key_ok = col >= first_valid
        for p in range(N_PAIRS):
            sl = slice(p * LANES, (p + 1) * LANES)
            q = proj_s[r0:r0 + Q_BLOCK, OFF_QA + p * LANES:OFF_QA + (p + 1) * LANES]
            q2 = jnp.concatenate([jnp.where(low, q, 0.0), jnp.where(low, 0.0, q)],
                                 axis=0).astype(jnp.bfloat16)
            kb = k_s[r0:r0 + BAND, sl]
            vb = v_s[r0:r0 + BAND, sl]
            s = lax.dot_general(q2, kb, _NT, preferred_element_type=jnp.float32)
            s = jnp.where(key_ok, s + bias_ref[p], NEG)
            m = jnp.max(s, axis=-1, keepdims=True)
            e = jnp.exp(s - m)
            den = jnp.sum(e, axis=-1, keepdims=True)
            o2 = _dot(e.astype(jnp.bfloat16), vb) / den
            o = jnp.where(low, o2[0:Q_BLOCK], o2[Q_BLOCK:2 * Q_BLOCK])
            mix_s[r0:r0 + Q_BLOCK, MIX_ATT + p * LANES:MIX_ATT + (p + 1) * LANES] = (
                o.astype(jnp.bfloat16))
    k_s[0:ts, :] = k_s[ts:2 * ts, :]
    v_s[0:ts, :] = v_s[ts:2 * ts, :]

    a = proj_s[:, OFF_CA:OFF_CA + D_CONV]
    gate = proj_s[:, OFF_CG:OFF_CG + D_CONV]
    glu_s[CONV_HIST:CONV_HIST + ts, :] = a * jax.nn.sigmoid(gate)
    base = CONV_HIST - (CONV_K - 1)
    rb = 64
    for r in range(0, ts, rb):
        acc = cw_ref[0:1, :] * glu_s[base + r:base + r + rb, :]
        for kk in range(1, CONV_K):
            acc = acc + cw_ref[kk:kk + 1, :] * glu_s[base + r + kk:base + r + kk + rb, :]
        y = acc + cb_ref[...]
        mu = jnp.mean(y, axis=-1, keepdims=True)
        yc = y - mu
        var = jnp.mean(yc * yc, axis=-1, keepdims=True)
        y = (yc * lax.rsqrt(var + EPS)) * lng_ref[...] + lnb_ref[...]
        y = y * jax.nn.sigmoid(y)
        y = _dot(y.astype(jnp.bfloat16), pww_ref[...]) + pwb_ref[...]
        mix_s[r:r + rb, MIX_CONV:MIX_CONV + D_CONV] = y.astype(jnp.bfloat16)
    glu_s[0:CONV_HIST, :] = glu_s[ts:ts + CONV_HIST, :]

    row2 = lax.broadcasted_iota(jnp.int32, (2 * CHUNK, LANES), 0)
    lane2 = lax.broadcasted_iota(jnp.int32, (2 * CHUNK, LANES), 1)
    own2 = (row2 < CHUNK) == (lane2 < HEAD_DIM)
    for p in range(N_PAIRS):
        for n in range(ts // CHUNK):
            rs = slice(n * CHUNK, (n + 1) * CHUNK)
            q = proj_s[rs, OFF_QR + p * LANES:OFF_QR + (p + 1) * LANES]
            k = proj_s[rs, OFF_KR + p * LANES:OFF_KR + (p + 1) * LANES]
            v = proj_s[rs, OFF_VR + p * LANES:OFF_VR + (p + 1) * LANES].astype(jnp.bfloat16)
            g = proj_s[rs, OFF_GR + p * LANES:OFF_GR + (p + 1) * LANES]
            cos = cos_ref[rs, :]
            sin = sin_ref[rs, :]
            qr = (q * cos + _swap_halves(q, first_half) * sin) * (HEAD_DIM ** -0.5)
            kr = k * cos + _swap_halves(k, first_half) * sin
            q2 = jnp.where(own2, jnp.concatenate([qr, qr], axis=0), 0.0)
            s = lax.dot_general(q2.astype(jnp.bfloat16), kr.astype(jnp.bfloat16), _NT,
                                preferred_element_type=jnp.float32)
            sd = (s * dmat_ref[p]).astype(jnp.bfloat16)
            qd = (q2 * qdec_ref[p]).astype(jnp.bfloat16)
            state = st_s[p]
            lhs = jnp.concatenate([qd, sd], axis=1)
            rhs = jnp.concatenate([state.astype(jnp.bfloat16), v], axis=0)
            o2 = jnp.where(own2, _dot(lhs, rhs), 0.0)
            mu = jnp.sum(o2, axis=-1, keepdims=True) * (1.0 / HEAD_DIM)
            cen = jnp.where(own2, o2 - mu, 0.0)
            var = jnp.sum(cen * cen, axis=-1, keepdims=True) * (1.0 / HEAD_DIM)
            y2 = cen * lax.rsqrt(var + EPS)
            y = (y2[0:CHUNK] + y2[CHUNK:2 * CHUNK]) * gn_ref[p]
            y = (g * jax.nn.sigmoid(g)) * y
            mix_s[rs, MIX_RET + p * LANES:MIX_RET + (p + 1) * LANES] = y.astype(jnp.bfloat16)
            kd = (kr * kdec_ref[p]).astype(jnp.bfloat16)
            kv = lax.dot_general(kd, v, _TN, preferred_element_type=jnp.float32)
            st_s[p] = cdec_ref[p] * state + jnp.where(own2, kv, 0.0)

    o_ref[0] = h_ref[0] + _dot(mix_s[...], w_o_ref[...])


def _ffn_kernel(h_ref, p_ref, gf_ref, w1_ref, w2_ref, gp_ref, wpg_ref, wple_ref, o_ref):
    x = h_ref[...]
    hn = _rms(x, gf_ref[...]).astype(jnp.bfloat16)
    acc = jnp.zeros_like(x)
    for j in range(0, D_FF, FF_BLOCK):
        u = jnp.maximum(_dot(hn, w1_ref[:, j:j + FF_BLOCK]), 0.0)
        acc = acc + _dot((u * u).astype(jnp.bfloat16), w2_ref[j:j + FF_BLOCK, :])
    h2 = x + acc
    gn = _rms(h2, gp_ref[...]).astype(jnp.bfloat16)
    gate = jax.nn.sigmoid(_dot(gn, wpg_ref[...]))
    ple = _dot(p_ref[...].astype(jnp.bfloat16), wple_ref[...])
    o_ref[...] = h2 + gate * ple


def _const_spec(shape):
    nd = len(shape)
    return pl.BlockSpec(shape, lambda *_: (0,) * nd, pipeline_mode=pl.Buffered(1))


def _mixer_layer(h, consts, lw):
    b, s, _ = h.shape
    nt = s // SEQ_TILE
    ts = SEQ_TILE
    operands = [
        h, lw["g_mix"], lw["w_in"], lw["qn_g"], lw["kn_g"], lw["bias"],
        lw["conv_w"], lw["conv_b"], lw["ln_g"], lw["ln_b"], lw["pw_w"], lw["pw_b"],
        consts["cos"], consts["sin"], consts["dmat"], consts["qdec"], consts["kdec"],
        consts["cdec"], lw["gn_g"], lw["w_o"],
    ]
    in_specs = [pl.BlockSpec((1, ts, D_MODEL), lambda bi, ti: (bi, ti, 0))]
    for op in operands[1:]:
        in_specs.append(_const_spec(op.shape))
    in_specs[12] = pl.BlockSpec((ts, LANES), lambda bi, ti: (ti, 0))
    in_specs[13] = pl.BlockSpec((ts, LANES), lambda bi, ti: (ti, 0))
    return pl.pallas_call(
        _mixer_kernel,
        grid=(b, nt),
        in_specs=in_specs,
        out_specs=pl.BlockSpec((1, ts, D_MODEL), lambda bi, ti: (bi, ti, 0)),
        out_shape=jax.ShapeDtypeStruct(h.shape, jnp.float32),
        scratch_shapes=[
            pltpu.VMEM((ts, D_MODEL), jnp.bfloat16),
            pltpu.VMEM((ts, D_IN), jnp.float32),
            pltpu.VMEM((2 * ts, D_ATT), jnp.bfloat16),
            pltpu.VMEM((2 * ts, D_ATT), jnp.bfloat16),
            pltpu.VMEM((CONV_HIST + ts, D_CONV), jnp.float32),
            pltpu.VMEM((N_PAIRS, LANES, LANES), jnp.float32),
            pltpu.VMEM((ts, D_MODEL), jnp.bfloat16),
        ],
        compiler_params=pltpu.CompilerParams(
            dimension_semantics=("arbitrary", "arbitrary"),
            vmem_limit_bytes=VMEM_LIMIT_BYTES),
        name="mixer_layer",
    )(*operands)


def _ffn_layer(h, p_i, lw):
    b, s, _ = h.shape
    t = b * s
    h2 = h.reshape(t, D_MODEL)
    p2 = p_i.reshape(t, D_PLE)
    operands = [h2, p2, lw["g_ffn"], lw["w1"], lw["w2"], lw["g_ple"], lw["w_pg"], lw["w_ple"]]
    in_specs = [pl.BlockSpec((FFN_TILE, D_MODEL), lambda i: (i, 0)),
                pl.BlockSpec((FFN_TILE, D_PLE), lambda i: (i, 0))]
    for op in operands[2:]:
        in_specs.append(_const_spec(op.shape))
    out = pl.pallas_call(
        _ffn_kernel,
        grid=(t // FFN_TILE,),
        in_specs=in_specs,
        out_specs=pl.BlockSpec((FFN_TILE, D_MODEL), lambda i: (i, 0)),
        out_shape=jax.ShapeDtypeStruct((t, D_MODEL), jnp.float32),
        compiler_params=pltpu.CompilerParams(
            dimension_semantics=("arbitrary",),
            vmem_limit_bytes=VMEM_LIMIT_BYTES),
        name="ffn_ple_layer",
    )(*operands)
    return out.reshape(b, s, D_MODEL)


def _pair_lanes(per_head):
    x = jnp.repeat(per_head[..., None], HEAD_DIM, axis=-1)
    x = x.reshape((N_PAIRS, 2) + x.shape[1:])
    return jnp.concatenate([x[:, 0], x[:, 1]], axis=-1)


def _retention_consts(s):
    f32 = jnp.float32
    pos = jnp.arange(s, dtype=f32)
    inv_freq = ROPE_THETA ** (-jnp.arange(0, HEAD_DIM, 2, dtype=f32) / HEAD_DIM)
    ang = pos[:, None] * inv_freq[None, :]
    cos, sin = jnp.cos(ang), jnp.sin(ang)
    cos4 = jnp.concatenate([cos, cos, cos, cos], axis=-1)
    sin4 = jnp.concatenate([-sin, sin, -sin, sin], axis=-1)
    log_gamma = jnp.log(1.0 - 2.0 ** (-5.0 - jnp.arange(H_RET, dtype=f32)))
    n = jnp.arange(CHUNK, dtype=f32)
    dist = jnp.abs(n[:, None] - n[None, :])
    d_intra = jnp.exp(log_gamma[:, None, None] * dist[None])
    dmat = d_intra.reshape(N_PAIRS, 2 * CHUNK, CHUNK)
    q_dec = jnp.exp(log_gamma[:, None] * (n + 1.0)[None, :])
    k_dec = jnp.exp(log_gamma[:, None] * (CHUNK - 1 - n)[None, :])
    chunk_decay = jnp.exp(log_gamma * CHUNK)
    qd = q_dec.reshape(N_PAIRS, 2 * CHUNK)
    qdec = jnp.broadcast_to(qd[:, :, None], (N_PAIRS, 2 * CHUNK, LANES))
    kdec = _pair_lanes(k_dec)
    cd = chunk_decay.reshape(N_PAIRS, 2)
    blk = jnp.kron(jnp.eye(2, dtype=f32), jnp.ones((HEAD_DIM, HEAD_DIM), f32))
    cdec = jnp.kron(cd[:, :, None] * jnp.eye(2, dtype=f32)[None],
                    jnp.ones((HEAD_DIM, HEAD_DIM), f32)) * blk[None]
    return dict(cos=cos4, sin=sin4, dmat=dmat, qdec=qdec, kdec=kdec, cdec=cdec)


def _attention_bias(rel_bias):
    i = jnp.arange(CHUNK)
    j = jnp.arange(BAND)
    tabs = []
    for half in range(2):
        band_col = j - half * CHUNK
        ok = (band_col >= 0) & (band_col < (N_LEFT_CHUNKS + 1) * CHUNK)
        rel = (N_LEFT_CHUNKS * CHUNK + i[:, None]) - band_col[None, :]
        idx = jnp.clip(rel, -REL_CLIP, REL_CLIP) + REL_CLIP
        b = rel_bias.astype(jnp.float32)[:, idx]
        tabs.append(jnp.where(ok[None, None, :], b, NEG))
    per_head = jnp.concatenate(tabs, axis=1)
    return per_head.reshape(N_PAIRS, 2 * Q_BLOCK, BAND)


def kernel(x, p, norm_mix_g, w_in, qn_g, kn_g, rel_bias, conv_w, conv_b, conv_ln_g,
           conv_ln_b, conv_pw_w, conv_pw_b, ret_gn_g, w_o, norm_ffn_g, w1, w2,
           norm_ple_g, w_pg, w_ple):
    depth = w_in.shape[0]
    s = x.shape[1]
    assert s % SEQ_TILE == 0 and (x.shape[0] * s) % FFN_TILE == 0
    bf16 = jnp.bfloat16
    consts = _retention_consts(s)
    h = x
    for i in range(depth):
        lw = dict(
            g_mix=norm_mix_g[i][None, :],
            w_in=w_in[i].astype(bf16),
            qn_g=jnp.tile(qn_g[i], 2)[None, :],
            kn_g=jnp.tile(kn_g[i], 2)[None, :],
            bias=_attention_bias(rel_bias[i]),
            conv_w=jnp.pad(conv_w[i], ((0, 1), (0, 0))),
            conv_b=conv_b[i][None, :],
            ln_g=conv_ln_g[i][None, :],
            ln_b=conv_ln_b[i][None, :],
            pw_w=conv_pw_w[i].astype(bf16),
            pw_b=conv_pw_b[i][None, :],
            gn_g=ret_gn_g[i].reshape(N_PAIRS, 1, LANES),
            w_o=w_o[i].astype(bf16),
            g_ffn=norm_ffn_g[i][None, :],
            w1=w1[i].astype(bf16),
            w2=w2[i].astype(bf16),
            g_ple=norm_ple_g[i][None, :],
            w_pg=w_pg[i].astype(bf16),
            w_ple=w_ple[i].astype(bf16),
        )
        h = _mixer_layer(h, consts, lw)
        h = _ffn_layer(h, p[i], lw)
    return h
```

```python
import numpy as np
import jax
import jax.numpy as jnp
from jax import lax
from jax.experimental import pallas as pl
from jax.experimental.pallas import tpu as pltpu

D_MODEL = 1024
CHUNK = 64
N_LEFT_CHUNKS = 8
HEAD_DIM = 64
D_ATT = 384
D_RET = 384
D_CONV = 256
H_ATT = D_ATT // HEAD_DIM
H_RET = D_RET // HEAD_DIM
CONV_K = 31
REL_CLIP = 128
D_FF = 4 * D_MODEL
D_PLE = 256
ROPE_THETA = 10000.0
EPS = 1e-6
D_IN = 3 * D_ATT + 2 * D_CONV + 4 * D_RET

OFF_QA, OFF_KA, OFF_VA = 0, D_ATT, 2 * D_ATT
OFF_CA = 3 * D_ATT
OFF_CG = OFF_CA + D_CONV
OFF_QR = OFF_CA + 2 * D_CONV
OFF_KR = OFF_QR + D_RET
OFF_VR = OFF_KR + D_RET
OFF_GR = OFF_VR + D_RET
MIX_ATT, MIX_CONV, MIX_RET = 0, D_ATT, D_ATT + D_CONV

LANES = 128
N_PAIRS = D_ATT // LANES
SEQ_TILE = N_LEFT_CHUNKS * CHUNK
Q_BLOCK = 2 * CHUNK
BAND = (N_LEFT_CHUNKS + 2) * CHUNK
CONV_HIST = 32
FFN_TILE = 512
FF_BLOCK = 512
NEG = -1e30
VMEM_LIMIT_BYTES = 56 * 1024 * 1024

_NT = (((1,), (1,)), ((), ()))
_TN = (((0,), (0,)), ((), ()))


def _rms(x, g):
    return (x * lax.rsqrt(jnp.mean(x * x, axis=-1, keepdims=True) + EPS)) * g


def _dot(a, b):
    return jnp.dot(a, b, preferred_element_type=jnp.float32)


def _pair_rms(x, low, g):
    x2 = x * x
    s0 = jnp.sum(jnp.where(low, x2, 0.0), axis=-1, keepdims=True)
    s1 = jnp.sum(jnp.where(low, 0.0, x2), axis=-1, keepdims=True)
    r0 = lax.rsqrt(s0 * (1.0 / HEAD_DIM) + EPS)
    r1 = lax.rsqrt(s1 * (1.0 / HEAD_DIM) + EPS)
    return (x * jnp.where(low, r0, r1)) * g


def _swap_halves(x, first_half):
    return jnp.where(first_half, pltpu.roll(x, 96, 1), pltpu.roll(x, 32, 1))


def _mixer_kernel(h_ref, g_ref, w_in_ref, qn_ref, kn_ref, bias_ref,
                  cw_ref, cb_ref, lng_ref, lnb_ref, pww_ref, pwb_ref,
                  cos_ref, sin_ref, dmat_ref, qdec_ref, kdec_ref, cdec_ref, gn_ref,
                  w_o_ref, o_ref,
                  xn_s, proj_s, k_s, v_s, glu_s, st_s, mix_s):
    t = pl.program_id(1)
    ts = SEQ_TILE

    @pl.when(t == 0)
    def _():
        k_s[0:ts, :] = jnp.zeros((ts, D_ATT), jnp.bfloat16)
        v_s[0:ts, :] = jnp.zeros((ts, D_ATT), jnp.bfloat16)
        glu_s[0:CONV_HIST, :] = jnp.zeros((CONV_HIST, D_CONV), jnp.float32)
        st_s[...] = jnp.zeros_like(st_s)

    xn_s[...] = _rms(h_ref[...], g_ref[...]).astype(jnp.bfloat16)
    for c0 in range(0, D_IN, 640):
        proj_s[:, c0:c0 + 640] = _dot(xn_s[...], w_in_ref[:, c0:c0 + 640])

    lane = lax.broadcasted_iota(jnp.int32, (1, LANES), 1)
    low = lane < HEAD_DIM
    first_half = (lane & 32) == 0

    for p in range(N_PAIRS):
        sl = slice(p * LANES, (p + 1) * LANES)
        q = proj_s[:, OFF_QA + p * LANES:OFF_QA + (p + 1) * LANES]
        k = proj_s[:, OFF_KA + p * LANES:OFF_KA + (p + 1) * LANES]
        qn = _pair_rms(q, low, qn_ref[...]) * (HEAD_DIM ** -0.5)
        proj_s[:, OFF_QA + p * LANES:OFF_QA + (p + 1) * LANES] = qn
        k_s[ts:2 * ts, sl] = _pair_rms(k, low, kn_ref[...]).astype(jnp.bfloat16)
    v_s[ts:2 * ts, :] = proj_s[:, OFF_VA:OFF_VA + D_ATT].astype(jnp.bfloat16)

    col = lax.broadcasted_iota(jnp.int32, (1, BAND), 1)
    for c2 in range(ts // Q_BLOCK):
        r0 = c2 * Q_BLOCK
        first_valid = jnp.where(t == 0, ts - r0, 0)
        key_ok = col >= first_valid
        for p in range(N_PAIRS):
            sl = slice(p * LANES, (p + 1) * LANES)
            q = proj_s[r0:r0 + Q_BLOCK, OFF_QA + p * LANES:OFF_QA + (p + 1) * LANES]
            q2 = jnp.concatenate([jnp.where(low, q, 0.0), jnp.where(low, 0.0, q)],
                                 axis=0).astype(jnp.bfloat16)
            kb = k_s[r0:r0 + BAND, sl]
            vb = v_s[r0:r0 + BAND, sl]
            s = lax.dot_general(q2, kb, _NT, preferred_element_type=jnp.float32)
            s = jnp.where(key_ok, s + bias_ref[p], NEG)
            m = jnp.max(s, axis=-1, keepdims=True)
            e = jnp.exp(s - m)
            den = jnp.sum(e, axis=-1, keepdims=True)
            o2 = _dot(e.astype(jnp.bfloat16), vb) / den
            o = jnp.where(low, o2[0:Q_BLOCK], o2[Q_BLOCK:2 * Q_BLOCK])
            mix_s[r0:r0 + Q_BLOCK, MIX_ATT + p * LANES:MIX_ATT + (p + 1) * LANES] = (
                o.astype(jnp.bfloat16))
    k_s[0:ts, :] = k_s[ts:2 * ts, :]
    v_s[0:ts, :] = v_s[ts:2 * ts, :]

    a = proj_s[:, OFF_CA:OFF_CA + D_CONV]
    gate = proj_s[:, OFF_CG:OFF_CG + D_CONV]
    glu_s[CONV_HIST:CONV_HIST + ts, :] = a * jax.nn.sigmoid(gate)
    base = CONV_HIST - (CONV_K - 1)
    rb = 64
    for r in range(0, ts, rb):
        acc = cw_ref[0:1, :] * glu_s[base + r:base + r + rb, :]
        for kk in range(1, CONV_K):
            acc = acc + cw_ref[kk:kk + 1, :] * glu_s[base + r + kk:base + r + kk + rb, :]
        y = acc + cb_ref[...]
        mu = jnp.mean(y, axis=-1, keepdims=True)
        yc = y - mu
        var = jnp.mean(yc * yc, axis=-1, keepdims=True)
        y = (yc * lax.rsqrt(var + EPS)) * lng_ref[...] + lnb_ref[...]
        y = y * jax.nn.sigmoid(y)
        y = _dot(y.astype(jnp.bfloat16), pww_ref[...]) + pwb_ref[...]
        mix_s[r:r + rb, MIX_CONV:MIX_CONV + D_CONV] = y.astype(jnp.bfloat16)
    glu_s[0:CONV_HIST, :] = glu_s[ts:ts + CONV_HIST, :]

    row2 = lax.broadcasted_iota(jnp.int32, (2 * CHUNK, LANES), 0)
    lane2 = lax.broadcasted_iota(jnp.int32, (2 * CHUNK, LANES), 1)
    own2 = (row2 < CHUNK) == (lane2 < HEAD_DIM)
    for p in range(N_PAIRS):
        for n in range(ts // CHUNK):
            rs = slice(n * CHUNK, (n + 1) * CHUNK)
            q = proj_s[rs, OFF_QR + p * LANES:OFF_QR + (p + 1) * LANES]
            k = proj_s[rs, OFF_KR + p * LANES:OFF_KR + (p + 1) * LANES]
            v = proj_s[rs, OFF_VR + p * LANES:OFF_VR + (p + 1) * LANES].astype(jnp.bfloat16)
            g = proj_s[rs, OFF_GR + p * LANES:OFF_GR + (p + 1) * LANES]
            cos = cos_ref[rs, :]
            sin = sin_ref[rs, :]
            qr = (q * cos + _swap_halves(q, first_half) * sin) * (HEAD_DIM ** -0.5)
            kr = k * cos + _swap_halves(k, first_half) * sin
            q2 = jnp.where(own2, jnp.concatenate([qr, qr], axis=0), 0.0)
            s = lax.dot_general(q2.astype(jnp.bfloat16), kr.astype(jnp.bfloat16), _NT,
                                preferred_element_type=jnp.float32)
            sd = (s * dmat_ref[p]).astype(jnp.bfloat16)
            qd = (q2 * qdec_ref[p]).astype(jnp.bfloat16)
            state = st_s[p]
            lhs = jnp.concatenate([qd, sd], axis=1)
            rhs = jnp.concatenate([state.astype(jnp.bfloat16), v], axis=0)
            o2 = jnp.where(own2, _dot(lhs, rhs), 0.0)
            mu = jnp.sum(o2, axis=-1, keepdims=True) * (1.0 / HEAD_DIM)
            cen = jnp.where(own2, o2 - mu, 0.0)
            var = jnp.sum(cen * cen, axis=-1, keepdims=True) * (1.0 / HEAD_DIM)
            y2 = cen * lax.rsqrt(var + EPS)
            y = (y2[0:CHUNK] + y2[CHUNK:2 * CHUNK]) * gn_ref[p]
            y = (g * jax.nn.sigmoid(g)) * y
            mix_s[rs, MIX_RET + p * LANES:MIX_RET + (p + 1) * LANES] = y.astype(jnp.bfloat16)
            kd = (kr * kdec_ref[p]).astype(jnp.bfloat16)
            kv = lax.dot_general(kd, v, _TN, preferred_element_type=jnp.float32)
            st_s[p] = cdec_ref[p] * state + jnp.where(own2, kv, 0.0)

    o_ref[...] = h_ref[...] + _dot(mix_s[...], w_o_ref[...])


def _ffn_kernel(h_ref, p_ref, gf_ref, w1_ref, w2_ref, gp_ref, wpg_ref, wple_ref, o_ref):
    x = h_ref[...]
    hn = _rms(x, gf_ref[...]).astype(jnp.bfloat16)
    acc = jnp.zeros_like(x)
    for j in range(0, D_FF, FF_BLOCK):
        u = jnp.maximum(_dot(hn, w1_ref[:, j:j + FF_BLOCK]), 0.0)
        acc = acc + _dot((u * u).astype(jnp.bfloat16), w2_ref[j:j + FF_BLOCK, :])
    h2 = x + acc
    gn = _rms(h2, gp_ref[...]).astype(jnp.bfloat16)
    gate = jax.nn.sigmoid(_dot(gn, wpg_ref[...]))
    ple = _dot(p_ref[...].astype(jnp.bfloat16), wple_ref[...])
    o_ref[...] = h2 + gate * ple


def _const_spec(shape):
    nd = len(shape)
    return pl.BlockSpec(shape, lambda *_: (0,) * nd, pipeline_mode=pl.Buffered(1))


def _layer_spec(shape, layer):
    nd = len(shape)
    return pl.BlockSpec((None,) + tuple(shape[1:]), lambda *_: (layer,) + (0,) * (nd - 1),
                        pipeline_mode=pl.Buffered(1))


def _mixer_layer(h, layer, prm, consts):
    b, s, _ = h.shape
    ts = SEQ_TILE
    stacked = [prm[k] for k in ("g_mix", "w_in", "qn_g", "kn_g", "bias", "conv_w", "conv_b",
                                "ln_g", "ln_b", "pw_w", "pw_b")]
    tables = [consts[k] for k in ("dmat", "qdec", "kdec", "cdec")]
    tile_spec = pl.BlockSpec((None, ts, D_MODEL), lambda bi, ti: (bi, ti, 0))
    rope_spec = pl.BlockSpec((ts, LANES), lambda bi, ti: (ti, 0))
    in_specs = ([tile_spec] + [_layer_spec(a.shape, layer) for a in stacked]
                + [rope_spec, rope_spec] + [_const_spec(a.shape) for a in tables]
                + [_layer_spec(prm["gn_g"].shape, layer), _layer_spec(prm["w_o"].shape, layer)])
    return pl.pallas_call(
        _mixer_kernel,
        grid=(b, s // ts),
        in_specs=in_specs,
        out_specs=tile_spec,
        out_shape=jax.ShapeDtypeStruct(h.shape, jnp.float32),
        scratch_shapes=[
            pltpu.VMEM((ts, D_MODEL), jnp.bfloat16),
            pltpu.VMEM((ts, D_IN), jnp.float32),
            pltpu.VMEM((2 * ts, D_ATT), jnp.bfloat16),
            pltpu.VMEM((2 * ts, D_ATT), jnp.bfloat16),
            pltpu.VMEM((CONV_HIST + ts, D_CONV), jnp.float32),
            pltpu.VMEM((N_PAIRS, LANES, LANES), jnp.float32),
            pltpu.VMEM((ts, D_MODEL), jnp.bfloat16),
        ],
        compiler_params=pltpu.CompilerParams(
            dimension_semantics=("arbitrary", "arbitrary"),
            vmem_limit_bytes=VMEM_LIMIT_BYTES),
        name="mixer_layer",
    )(h, *stacked, consts["cos"], consts["sin"], *tables, prm["gn_g"], prm["w_o"])


def _ffn_layer(h, layer, prm):
    b, s, _ = h.shape
    t = b * s
    stacked = [prm[k] for k in ("g_ffn", "w1", "w2", "g_ple", "w_pg", "w_ple")]
    in_specs = ([pl.BlockSpec((FFN_TILE, D_MODEL), lambda i: (i, 0)),
                 pl.BlockSpec((None, FFN_TILE, D_PLE), lambda i: (layer, i, 0))]
                + [_layer_spec(a.shape, layer) for a in stacked])
    out = pl.pallas_call(
        _ffn_kernel,
        grid=(t // FFN_TILE,),
        in_specs=in_specs,
        out_specs=pl.BlockSpec((FFN_TILE, D_MODEL), lambda i: (i, 0)),
        out_shape=jax.ShapeDtypeStruct((t, D_MODEL), jnp.float32),
        compiler_params=pltpu.CompilerParams(
            dimension_semantics=("arbitrary",),
            vmem_limit_bytes=VMEM_LIMIT_BYTES),
        name="ffn_ple_layer",
    )(h.reshape(t, D_MODEL), prm["p"], *stacked)
    return out.reshape(b, s, D_MODEL)


def _pair_lanes(per_head):
    x = np.repeat(per_head[..., None], HEAD_DIM, axis=-1)
    x = x.reshape((N_PAIRS, 2) + x.shape[1:])
    return np.concatenate([x[:, 0], x[:, 1]], axis=-1)


def _retention_consts(s):
    pos = np.arange(s, dtype=np.float64)
    inv_freq = ROPE_THETA ** (-np.arange(0, HEAD_DIM, 2, dtype=np.float64) / HEAD_DIM)
    ang = pos[:, None] * inv_freq[None, :]
    cos, sin = np.cos(ang), np.sin(ang)
    cos4 = np.concatenate([cos, cos, cos, cos], axis=-1)
    sin4 = np.concatenate([-sin, sin, -sin, sin], axis=-1)
    log_gamma = np.log(1.0 - 2.0 ** (-5.0 - np.arange(H_RET, dtype=np.float64)))
    n = np.arange(CHUNK, dtype=np.float64)
    dist = np.abs(n[:, None] - n[None, :])
    d_intra = np.exp(log_gamma[:, None, None] * dist[None])
    dmat = d_intra.reshape(N_PAIRS, 2 * CHUNK, CHUNK)
    q_dec = np.exp(log_gamma[:, None] * (n + 1.0)[None, :])
    k_dec = np.exp(log_gamma[:, None] * (CHUNK - 1 - n)[None, :])
    chunk_decay = np.exp(log_gamma * CHUNK).reshape(N_PAIRS, 2)
    qdec = np.broadcast_to(q_dec.reshape(N_PAIRS, 2 * CHUNK)[:, :, None],
                           (N_PAIRS, 2 * CHUNK, LANES))
    kdec = _pair_lanes(k_dec)
    ones = np.ones((HEAD_DIM, HEAD_DIM))
    cdec = np.stack([np.kron(np.diag(cd), ones) for cd in chunk_decay])
    tabs = dict(cos=cos4, sin=sin4, dmat=dmat, qdec=qdec, kdec=kdec, cdec=cdec)
    return {k: jnp.asarray(np.ascontiguousarray(v), dtype=jnp.float32) for k, v in tabs.items()}


def _attention_bias(rel_bias):
    lead = rel_bias.shape[:-1]
    own = (N_LEFT_CHUNKS + 1) * CHUNK
    n_m = own + CHUNK - 1
    n_far = N_LEFT_CHUNKS * CHUNK + CHUNK - REL_CLIP
    n_near = n_m - n_far
    rb = rel_bias.astype(jnp.float32)
    far = jnp.broadcast_to(rb[..., 2 * REL_CLIP:], lead + (n_far,))
    near = rb[..., 2 * REL_CLIP - n_near:2 * REL_CLIP][..., ::-1]
    vec = jnp.concatenate([far, near, jnp.zeros(lead + (1,), jnp.float32)], axis=-1)
    skew = jnp.tile(vec, CHUNK)[..., :CHUNK * n_m].reshape(lead + (CHUNK, n_m))
    band = skew[..., CHUNK - 1:CHUNK - 1 + own]
    neg = jnp.full(lead + (CHUNK, BAND - own), NEG, jnp.float32)
    per_head = jnp.concatenate([jnp.concatenate([band, neg], axis=-1),
                                jnp.concatenate([neg, band], axis=-1)], axis=-2)
    return per_head.reshape(lead[0], N_PAIRS, 2 * Q_BLOCK, BAND)


def kernel(x, p, norm_mix_g, w_in, qn_g, kn_g, rel_bias, conv_w, conv_b, conv_ln_g,
           conv_ln_b, conv_pw_w, conv_pw_b, ret_gn_g, w_o, norm_ffn_g, w1, w2,
           norm_ple_g, w_pg, w_ple):
    depth = w_in.shape[0]
    b, s, _ = x.shape
    assert s % SEQ_TILE == 0 and (b * s) % FFN_TILE == 0
    bf16 = jnp.bfloat16
    row = lambda a: a[:, None, :]
    prm = dict(
        g_mix=row(norm_mix_g), w_in=w_in.astype(bf16),
        qn_g=row(jnp.tile(qn_g, (1, 2))), kn_g=row(jnp.tile(kn_g, (1, 2))),
        bias=_attention_bias(rel_bias),
        conv_w=jnp.pad(conv_w, ((0, 0), (0, 1), (0, 0))), conv_b=row(conv_b),
        ln_g=row(conv_ln_g), ln_b=row(conv_ln_b),
        pw_w=conv_pw_w.astype(bf16), pw_b=row(conv_pw_b),
        gn_g=ret_gn_g.reshape(depth, N_PAIRS, 1, LANES), w_o=w_o.astype(bf16),
        g_ffn=row(norm_ffn_g), w1=w1.astype(bf16), w2=w2.astype(bf16),
        g_ple=row(norm_ple_g), w_pg=w_pg.astype(bf16), w_ple=w_ple.astype(bf16),
        p=p.reshape(depth, b * s, D_PLE),
    )
    consts = _retention_consts(s)
    h = x
    for layer in range(depth):
        h = _mixer_layer(h, layer, prm, consts)
        h = _ffn_layer(h, layer, prm)
    return h
```

```python
import math
from functools import partial

import numpy as np
import jax
import jax.numpy as jnp
from jax import lax
from jax.experimental import pallas as pl
from jax.experimental.pallas import tpu as pltpu

D_MODEL = 1024
CHUNK = 64
N_LEFT_CHUNKS = 8
HEAD_DIM = 64
D_ATT = 384
D_RET = 384
D_CONV = 256
H_ATT = D_ATT // HEAD_DIM
H_RET = D_RET // HEAD_DIM
CONV_K = 31
REL_CLIP = 128
D_FF = 4 * D_MODEL
D_PLE = 256
ROPE_THETA = 10000.0
EPS = 1e-6
D_IN = 3 * D_ATT + 2 * D_CONV + 4 * D_RET

OFF_QA, OFF_KA, OFF_VA = 0, D_ATT, 2 * D_ATT
OFF_CA = 3 * D_ATT
OFF_CG = OFF_CA + D_CONV
OFF_QR = OFF_CA + 2 * D_CONV
OFF_KR = OFF_QR + D_RET
OFF_VR = OFF_KR + D_RET
OFF_GR = OFF_VR + D_RET
MIX_ATT, MIX_CONV, MIX_RET = 0, D_ATT, D_ATT + D_CONV

LANES = 128
SUBLANES = 8
N_PAIRS = D_ATT // LANES
SEQ_TILE = N_LEFT_CHUNKS * CHUNK
Q_BLOCK = 2 * CHUNK
BAND = (N_LEFT_CHUNKS + 2) * CHUNK
CONV_HIST = 32
CONV_ROWS = 64
NORM_ROWS = 64
PROJ_COLS = 640
NEXT_COLS = 256
ATT_PIECES_PER_ROW_BLOCK = 2
OUT_ROWS = 128
FFN_TILE = 512
FF_BLOCK = 512
NEG = -1e30
LOG2E = math.log2(math.e)
VMEM_LIMIT_BYTES = 56 * 1024 * 1024

_NT = (((1,), (1,)), ((), ()))
_TN = (((0,), (0,)), ((), ()))


def _rms(x, g):
    return (x * lax.rsqrt(jnp.mean(x * x, axis=-1, keepdims=True) + EPS)) * g


def _dot(a, b):
    return jnp.dot(a, b, preferred_element_type=jnp.float32)


def _pair_sum(x, low):
    s0 = jnp.sum(jnp.where(low, x, 0.0), axis=-1, keepdims=True)
    s1 = jnp.sum(jnp.where(low, 0.0, x), axis=-1, keepdims=True)
    return jnp.where(low, s0, s1)


def _pair_rms(x, low, g):
    ms = _pair_sum(x * x, low) * (1.0 / HEAD_DIM)
    return (x * lax.rsqrt(ms + EPS)) * g


def _swap_halves(x, first_half):
    return jnp.where(first_half, pltpu.roll(x, 96, 1), pltpu.roll(x, 32, 1))


def _mixer_tile(proj, nxt, t, hc_ref, hn_ref, g_ref, w_in_ref, qn_ref, kn_ref, bias_ref,
                cw_ref, cb_ref, lng_ref, lnb_ref, pww_ref, pwb_ref,
                cos_ref, sin_ref, dmat_ref, qdec_ref, kdec_ref, cdec_ref, gn_ref,
                w_o_ref, o_ref, xn_s, k_s, v_s, glu_s, st_s, mix_s):
    ts = SEQ_TILE
    xn_s[...] = _rms(hn_ref[...], g_ref[...]).astype(jnp.bfloat16)
    pieces = list(range(0, D_IN, NEXT_COLS))

    def next_pieces(n):
        for _ in range(n):
            if pieces:
                c0 = pieces.pop(0)
                c1 = min(c0 + NEXT_COLS, D_IN)
                nxt[:, c0:c1] = _dot(xn_s[...], w_in_ref[:, c0:c1])

    lane = lax.broadcasted_iota(jnp.int32, (1, LANES), 1)
    low = lane < HEAD_DIM
    first_half = (lane & 32) == 0

    for p in range(N_PAIRS):
        for r0 in range(0, ts, NORM_ROWS):
            rs = slice(r0, r0 + NORM_ROWS)
            qc = slice(OFF_QA + p * LANES, OFF_QA + (p + 1) * LANES)
            kc = slice(OFF_KA + p * LANES, OFF_KA + (p + 1) * LANES)
            proj[rs, qc] = _pair_rms(proj[rs, qc], low, qn_ref[...]) * (HEAD_DIM ** -0.5 * LOG2E)
            k_s[ts + r0:ts + r0 + NORM_ROWS, p * LANES:(p + 1) * LANES] = (
                _pair_rms(proj[rs, kc], low, kn_ref[...]).astype(jnp.bfloat16))
    for r0 in range(0, ts, NORM_ROWS):
        v_s[ts + r0:ts + r0 + NORM_ROWS, :] = (
            proj[r0:r0 + NORM_ROWS, OFF_VA:OFF_VA + D_ATT].astype(jnp.bfloat16))

    col = lax.broadcasted_iota(jnp.int32, (1, BAND), 1)
    for c2 in range(ts // Q_BLOCK):
        r0 = c2 * Q_BLOCK
        first_valid = jnp.where(t == 0, ts - r0, 0)
        key_ok = col >= first_valid
        for p in range(N_PAIRS):
            sl = slice(p * LANES, (p + 1) * LANES)
            q = proj[r0:r0 + Q_BLOCK, OFF_QA + p * LANES:OFF_QA + (p + 1) * LANES]
            q2 = jnp.concatenate([jnp.where(low, q, 0.0), jnp.where(low, 0.0, q)],
                                 axis=0).astype(jnp.bfloat16)
            kb = k_s[r0:r0 + BAND, sl]
            vb = v_s[r0:r0 + BAND, sl]
            s = lax.dot_general(q2, kb, _NT, preferred_element_type=jnp.float32)
            s = jnp.where(key_ok, s + bias_ref[p], NEG)
            m = jnp.max(s, axis=-1, keepdims=True)
            e = jnp.exp2(s - m)
            den = jnp.sum(e, axis=-1, keepdims=True)
            o2 = _dot(e.astype(jnp.bfloat16), vb) / den
            o = jnp.where(low, o2[0:Q_BLOCK], o2[Q_BLOCK:2 * Q_BLOCK])
            mix_s[r0:r0 + Q_BLOCK, MIX_ATT + p * LANES:MIX_ATT + (p + 1) * LANES] = (
                o.astype(jnp.bfloat16))
            if p < ATT_PIECES_PER_ROW_BLOCK:
                next_pieces(1)
    k_s[0:ts, :] = k_s[ts:2 * ts, :]
    v_s[0:ts, :] = v_s[ts:2 * ts, :]

    a = proj[:, OFF_CA:OFF_CA + D_CONV]
    gate = proj[:, OFF_CG:OFF_CG + D_CONV]
    glu_s[CONV_HIST:CONV_HIST + ts, :] = a * jax.nn.sigmoid(gate)
    base = CONV_HIST - (CONV_K - 1)

    def conv_block(r):
        acc = None
        for rho in range(SUBLANES):
            rows = CONV_ROWS + (SUBLANES if rho else 0)
            z = None
            for kk in range(CONV_K):
                if (base + kk) % SUBLANES != rho:
                    continue
                off = r + base + kk - rho
                term = cw_ref[kk:kk + 1, :] * glu_s[off:off + rows, :]
                z = term if z is None else z + term
            z = z[rho:rho + CONV_ROWS]
            acc = z if acc is None else acc + z
        y = acc + cb_ref[...]
        mu = jnp.mean(y, axis=-1, keepdims=True)
        yc = y - mu
        var = jnp.mean(yc * yc, axis=-1, keepdims=True)
        y = (yc * lax.rsqrt(var + EPS)) * lng_ref[...] + lnb_ref[...]
        y = y * jax.nn.sigmoid(y)
        y = _dot(y.astype(jnp.bfloat16), pww_ref[...]) + pwb_ref[...]
        mix_s[r:r + CONV_ROWS, MIX_CONV:MIX_CONV + D_CONV] = y.astype(jnp.bfloat16)

    row2 = lax.broadcasted_iota(jnp.int32, (2 * CHUNK, LANES), 0)
    lane2 = lax.broadcasted_iota(jnp.int32, (2 * CHUNK, LANES), 1)
    own2 = (row2 < CHUNK) == (lane2 < HEAD_DIM)
    for n in range(ts // CHUNK):
        rs = slice(n * CHUNK, (n + 1) * CHUNK)
        conv_block(n * CHUNK)
        cos = cos_ref[rs, :]
        sin = sin_ref[rs, :]
        for p in range(N_PAIRS):
            q = proj[rs, OFF_QR + p * LANES:OFF_QR + (p + 1) * LANES]
            k = proj[rs, OFF_KR + p * LANES:OFF_KR + (p + 1) * LANES]
            v = proj[rs, OFF_VR + p * LANES:OFF_VR + (p + 1) * LANES]
            g = proj[rs, OFF_GR + p * LANES:OFF_GR + (p + 1) * LANES]
            qr = (q * cos + _swap_halves(q, first_half) * sin) * (HEAD_DIM ** -0.5)
            kr = k * cos + _swap_halves(k, first_half) * sin
            k2 = jnp.where(own2, jnp.concatenate([kr, kr], axis=0), 0.0).astype(jnp.bfloat16)
            v2 = jnp.where(own2, jnp.concatenate([v, v], axis=0), 0.0).astype(jnp.bfloat16)
            s = lax.dot_general(qr.astype(jnp.bfloat16), k2, _NT,
                                preferred_element_type=jnp.float32)
            sd = (s * dmat_ref[p]).astype(jnp.bfloat16)
            qd = (qr * qdec_ref[p]).astype(jnp.bfloat16)
            state = st_s[p]
            lhs = jnp.concatenate([qd, sd], axis=1)
            rhs = jnp.concatenate([state.astype(jnp.bfloat16), v2], axis=0)
            o = _dot(lhs, rhs)
            cen = o - _pair_sum(o, low) * (1.0 / HEAD_DIM)
            var = _pair_sum(cen * cen, low) * (1.0 / HEAD_DIM)
            y = (cen * lax.rsqrt(var + EPS)) * gn_ref[p]
            y = (g * jax.nn.sigmoid(g)) * y
            mix_s[rs, MIX_RET + p * LANES:MIX_RET + (p + 1) * LANES] = y.astype(jnp.bfloat16)
            kd = (kr * kdec_ref[p]).astype(jnp.bfloat16)
            kv = lax.dot_general(kd, v.astype(jnp.bfloat16), _TN,
                                 preferred_element_type=jnp.float32)
            st_s[p] = cdec_ref[p] * state + jnp.where(own2, kv, 0.0)
        if (n + 1) % (OUT_ROWS // CHUNK) == 0:
            ro = slice((n + 1) * CHUNK - OUT_ROWS, (n + 1) * CHUNK)
            o_ref[ro, :] = hc_ref[ro, :] + _dot(mix_s[ro, :], w_o_ref[...])
        else:
            next_pieces(-(-len(pieces) // ((ts // CHUNK - n) // 2)))
    glu_s[0:CONV_HIST, :] = glu_s[ts:ts + CONV_HIST, :]


def _mixer_kernel(hc_ref, hn_ref, g_ref, w_in_ref, *rest, tiles_per_seq):
    refs = rest[:-7]
    xn_s, proj_s, k_s, v_s, glu_s, st_s, mix_s = rest[-7:]
    step = pl.program_id(0)
    t = lax.rem(step, tiles_per_seq)
    slot = lax.rem(step, 2)
    ts = SEQ_TILE

    @pl.when(step == 0)
    def _():
        xn_s[...] = _rms(hc_ref[...], g_ref[...]).astype(jnp.bfloat16)
        for c0 in range(0, D_IN, PROJ_COLS):
            proj_s[0, :, c0:c0 + PROJ_COLS] = _dot(xn_s[...], w_in_ref[:, c0:c0 + PROJ_COLS])

    @pl.when(t == 0)
    def _():
        k_s[0:ts, :] = jnp.zeros((ts, D_ATT), jnp.bfloat16)
        v_s[0:ts, :] = jnp.zeros((ts, D_ATT), jnp.bfloat16)
        glu_s[0:CONV_HIST, :] = jnp.zeros((CONV_HIST, D_CONV), jnp.float32)
        st_s[...] = jnp.zeros_like(st_s)

    for parity in range(2):
        @pl.when(slot == parity)
        def _():
            _mixer_tile(proj_s.at[parity], proj_s.at[1 - parity], t, hc_ref, hn_ref, g_ref,
                        w_in_ref, *refs, xn_s, k_s, v_s, glu_s, st_s, mix_s)


def _ffn_kernel(h_ref, p_ref, gf_ref, w1_ref, w2_ref, gp_ref, wpg_ref, wple_ref, o_ref):
    x = h_ref[...]
    hn = _rms(x, gf_ref[...]).astype(jnp.bfloat16)
    acc = jnp.zeros_like(x)
    for j in range(0, D_FF, FF_BLOCK):
        u = jnp.maximum(_dot(hn, w1_ref[:, j:j + FF_BLOCK]), 0.0)
        acc = acc + _dot((u * u).astype(jnp.bfloat16), w2_ref[j:j + FF_BLOCK, :])
    h2 = x + acc
    gn = _rms(h2, gp_ref[...]).astype(jnp.bfloat16)
    gate = jax.nn.sigmoid(_dot(gn, wpg_ref[...]))
    ple = _dot(p_ref[...].astype(jnp.bfloat16), wple_ref[...])
    o_ref[...] = h2 + gate * ple


def _const_spec(shape):
    nd = len(shape)
    return pl.BlockSpec(shape, lambda *_: (0,) * nd, pipeline_mode=pl.Buffered(1))


def _layer_spec(shape, layer):
    nd = len(shape)
    return pl.BlockSpec((None,) + tuple(shape[1:]), lambda *_: (layer,) + (0,) * (nd - 1),
                        pipeline_mode=pl.Buffered(1))


def _mixer_layer(h, layer, prm, consts):
    b, s, _ = h.shape
    ts = SEQ_TILE
    nt = s // ts
    n_steps = b * nt
    h3 = h.reshape(n_steps, ts, D_MODEL)
    stacked = [prm[k] for k in ("g_mix", "w_in", "qn_g", "kn_g", "bias", "conv_w", "conv_b",
                                "ln_g", "ln_b", "pw_w", "pw_b")]
    tables = [consts[k] for k in ("dmat", "qdec", "kdec", "cdec")]
    cur_spec = pl.BlockSpec((None, ts, D_MODEL), lambda i: (i, 0, 0))
    next_spec = pl.BlockSpec((None, ts, D_MODEL),
                             lambda i: (jnp.minimum(i + 1, n_steps - 1), 0, 0))
    rope_spec = pl.BlockSpec((ts, LANES), lambda i: (lax.rem(i, nt), 0))
    in_specs = ([cur_spec, next_spec] + [_layer_spec(a.shape, layer) for a in stacked]
                + [rope_spec, rope_spec] + [_const_spec(a.shape) for a in tables]
                + [_layer_spec(prm["gn_g"].shape, layer), _layer_spec(prm["w_o"].shape, layer)])
    out = pl.pallas_call(
        partial(_mixer_kernel, tiles_per_seq=nt),
        grid=(n_steps,),
        in_specs=in_specs,
        out_specs=cur_spec,
        out_shape=jax.ShapeDtypeStruct(h3.shape, jnp.float32),
        scratch_shapes=[
            pltpu.VMEM((ts, D_MODEL), jnp.bfloat16),
            pltpu.VMEM((2, ts, D_IN), jnp.float32),
            pltpu.VMEM((2 * ts, D_ATT), jnp.bfloat16),
            pltpu.VMEM((2 * ts, D_ATT), jnp.bfloat16),
            pltpu.VMEM((CONV_HIST + ts, D_CONV), jnp.float32),
            pltpu.VMEM((N_PAIRS, LANES, LANES), jnp.float32),
            pltpu.VMEM((ts, D_MODEL), jnp.bfloat16),
        ],
        compiler_params=pltpu.CompilerParams(
            dimension_semantics=("arbitrary",),
            vmem_limit_bytes=VMEM_LIMIT_BYTES),
        name="mixer_layer",
    )(h3, h3, *stacked, consts["cos"], consts["sin"], *tables, prm["gn_g"], prm["w_o"])
    return out.reshape(b, s, D_MODEL)


def _ffn_layer(h, layer, prm):
    b, s, _ = h.shape
    t = b * s
    stacked = [prm[k] for k in ("g_ffn", "w1", "w2", "g_ple", "w_pg", "w_ple")]
    in_specs = ([pl.BlockSpec((FFN_TILE, D_MODEL), lambda i: (i, 0)),
                 pl.BlockSpec((None, FFN_TILE, D_PLE), lambda i: (layer, i, 0))]
                + [_layer_spec(a.shape, layer) for a in stacked])
    out = pl.pallas_call(
        _ffn_kernel,
        grid=(t // FFN_TILE,),
        in_specs=in_specs,
        out_specs=pl.BlockSpec((FFN_TILE, D_MODEL), lambda i: (i, 0)),
        out_shape=jax.ShapeDtypeStruct((t, D_MODEL), jnp.float32),
        compiler_params=pltpu.CompilerParams(
            dimension_semantics=("arbitrary",),
            vmem_limit_bytes=VMEM_LIMIT_BYTES),
        name="ffn_ple_layer",
    )(h.reshape(t, D_MODEL), prm["p"], *stacked)
    return out.reshape(b, s, D_MODEL)


def _pair_lanes(per_head):
    x = np.repeat(per_head[..., None], HEAD_DIM, axis=-1)
    x = x.reshape((N_PAIRS, 2) + x.shape[1:])
    return np.concatenate([x[:, 0], x[:, 1]], axis=-1)


def _retention_consts(s):
    pos = np.arange(s, dtype=np.float64)
    inv_freq = ROPE_THETA ** (-np.arange(0, HEAD_DIM, 2, dtype=np.float64) / HEAD_DIM)
    ang = pos[:, None] * inv_freq[None, :]
    cos, sin = np.cos(ang), np.sin(ang)
    cos4 = np.concatenate([cos, cos, cos, cos], axis=-1)
    sin4 = np.concatenate([-sin, sin, -sin, sin], axis=-1)
    log_gamma = np.log(1.0 - 2.0 ** (-5.0 - np.arange(H_RET, dtype=np.float64)))
    n = np.arange(CHUNK, dtype=np.float64)
    dist = np.abs(n[:, None] - n[None, :])
    d_intra = np.exp(log_gamma[:, None, None] * dist[None])
    d_pair = d_intra.reshape(N_PAIRS, 2, CHUNK, CHUNK)
    dmat = np.concatenate([d_pair[:, 0], d_pair[:, 1]], axis=-1)
    q_dec = np.exp(log_gamma[:, None] * (n + 1.0)[None, :])
    k_dec = np.exp(log_gamma[:, None] * (CHUNK - 1 - n)[None, :])
    chunk_decay = np.exp(log_gamma * CHUNK).reshape(N_PAIRS, 2)
    qdec = _pair_lanes(q_dec)
    kdec = _pair_lanes(k_dec)
    ones = np.ones((HEAD_DIM, HEAD_DIM))
    cdec = np.stack([np.kron(np.diag(cd), ones) for cd in chunk_decay])
    tabs = dict(cos=cos4, sin=sin4, dmat=dmat, qdec=qdec, kdec=kdec, cdec=cdec)
    return {k: jnp.asarray(np.ascontiguousarray(v), dtype=jnp.float32) for k, v in tabs.items()}


def _attention_bias(rel_bias):
    lead = rel_bias.shape[:-1]
    own = (N_LEFT_CHUNKS + 1) * CHUNK
    n_m = own + CHUNK - 1
    n_far = N_LEFT_CHUNKS * CHUNK + CHUNK - REL_CLIP
    n_near = n_m - n_far
    rb = rel_bias.astype(jnp.float32) * LOG2E
    far = jnp.broadcast_to(rb[..., 2 * REL_CLIP:], lead + (n_far,))
    near = rb[..., 2 * REL_CLIP - n_near:2 * REL_CLIP][..., ::-1]
    vec = jnp.concatenate([far, near, jnp.zeros(lead + (1,), jnp.float32)], axis=-1)
    skew = jnp.tile(vec, CHUNK)[..., :CHUNK * n_m].reshape(lead + (CHUNK, n_m))
    band = skew[..., CHUNK - 1:CHUNK - 1 + own]
    neg = jnp.full(lead + (CHUNK, BAND - own), NEG, jnp.float32)
    per_head = jnp.concatenate([jnp.concatenate([band, neg], axis=-1),
                                jnp.concatenate([neg, band], axis=-1)], axis=-2)
    return per_head.reshape(lead[0], N_PAIRS, 2 * Q_BLOCK, BAND)


def kernel(x, p, norm_mix_g, w_in, qn_g, kn_g, rel_bias, conv_w, conv_b, conv_ln_g,
           conv_ln_b, conv_pw_w, conv_pw_b, ret_gn_g, w_o, norm_ffn_g, w1, w2,
           norm_ple_g, w_pg, w_ple):
    depth = w_in.shape[0]
    b, s, _ = x.shape
    assert s % SEQ_TILE == 0 and (b * s) % FFN_TILE == 0
    bf16 = jnp.bfloat16
    row = lambda a: a[:, None, :]
    prm = dict(
        g_mix=row(norm_mix_g), w_in=w_in.astype(bf16),
        qn_g=row(jnp.tile(qn_g, (1, 2))), kn_g=row(jnp.tile(kn_g, (1, 2))),
        bias=_attention_bias(rel_bias),
        conv_w=jnp.pad(conv_w, ((0, 0), (0, 1), (0, 0))), conv_b=row(conv_b),
        ln_g=row(conv_ln_g), ln_b=row(conv_ln_b),
        pw_w=conv_pw_w.astype(bf16), pw_b=row(conv_pw_b),
        gn_g=ret_gn_g.reshape(depth, N_PAIRS, 1, LANES), w_o=w_o.astype(bf16),
        g_ffn=row(norm_ffn_g), w1=w1.astype(bf16), w2=w2.astype(bf16),
        g_ple=row(norm_ple_g), w_pg=w_pg.astype(bf16), w_ple=w_ple.astype(bf16),
        p=p.reshape(depth, b * s, D_PLE),
    )
    consts = _retention_consts(s)
    h = x
    for layer in range(depth):
        h = _mixer_layer(h, layer, prm, consts)
        h = _ffn_layer(h, layer, prm)
    return h
```

```python
import math
from functools import partial

import numpy as np
import jax
import jax.numpy as jnp
from jax import lax
from jax.experimental import pallas as pl
from jax.experimental.pallas import tpu as pltpu

D_MODEL = 1024
CHUNK = 64
N_LEFT_CHUNKS = 8
HEAD_DIM = 64
D_ATT = 384
D_RET = 384
D_CONV = 256
H_ATT = D_ATT // HEAD_DIM
H_RET = D_RET // HEAD_DIM
CONV_K = 31
REL_CLIP = 128
D_FF = 4 * D_MODEL
D_PLE = 256
ROPE_THETA = 10000.0
EPS = 1e-6
D_IN = 3 * D_ATT + 2 * D_CONV + 4 * D_RET

OFF_QA, OFF_KA, OFF_VA = 0, D_ATT, 2 * D_ATT
OFF_CA = 3 * D_ATT
OFF_CG = OFF_CA + D_CONV
OFF_QR = OFF_CA + 2 * D_CONV
OFF_KR = OFF_QR + D_RET
OFF_VR = OFF_KR + D_RET
OFF_GR = OFF_VR + D_RET
MIX_ATT, MIX_CONV, MIX_RET = 0, D_ATT, D_ATT + D_CONV

LANES = 128
SUBLANES = 8
N_PAIRS = D_ATT // LANES
SEQ_TILE = N_LEFT_CHUNKS * CHUNK
Q_BLOCK = 2 * CHUNK
BAND = (N_LEFT_CHUNKS + 2) * CHUNK
CONV_HIST = 32
CONV_ROWS = 64
NORM_ROWS = 64
PROJ_COLS = 640
FFN_TILE = 512
FF_BLOCK = 512
NEG = -1e30
LOG2E = math.log2(math.e)
VMEM_LIMIT_BYTES = 56 * 1024 * 1024

_NT = (((1,), (1,)), ((), ()))
_TN = (((0,), (0,)), ((), ()))


def _rms(x, g):
    return (x * lax.rsqrt(jnp.mean(x * x, axis=-1, keepdims=True) + EPS)) * g


def _dot(a, b):
    return jnp.dot(a, b, preferred_element_type=jnp.float32)


def _pair_sum(x, low):
    s0 = jnp.sum(jnp.where(low, x, 0.0), axis=-1, keepdims=True)
    s1 = jnp.sum(jnp.where(low, 0.0, x), axis=-1, keepdims=True)
    return jnp.where(low, s0, s1)


def _pair_rms(x, low, g):
    ms = _pair_sum(x * x, low) * (1.0 / HEAD_DIM)
    return (x * lax.rsqrt(ms + EPS)) * g


def _swap_halves(x, first_half):
    return jnp.where(first_half, pltpu.roll(x, 96, 1), pltpu.roll(x, 32, 1))


def _mixer_kernel(hc_ref, g_ref, w_in_ref, qn_ref, kn_ref, bias_ref,
                  cw_ref, cb_ref, lng_ref, lnb_ref, pww_ref, pwb_ref,
                  cos_ref, sin_ref, dmat_ref, qdec_ref, kdec_ref, cdec_ref, gn_ref,
                  w_o_ref, o_ref, xn_s, proj, k_s, v_s, glu_s, st_s, mix_s, *, tiles_per_seq):
    ts = SEQ_TILE
    t = lax.rem(pl.program_id(0), tiles_per_seq)

    @pl.when(t == 0)
    def _():
        k_s[0:ts, :] = jnp.zeros((ts, D_ATT), jnp.bfloat16)
        v_s[0:ts, :] = jnp.zeros((ts, D_ATT), jnp.bfloat16)
        glu_s[0:CONV_HIST, :] = jnp.zeros((CONV_HIST, D_CONV), jnp.float32)
        st_s[...] = jnp.zeros_like(st_s)

    xn_s[...] = _rms(hc_ref[...], g_ref[...]).astype(jnp.bfloat16)
    for c0 in range(0, D_IN, PROJ_COLS):
        proj[:, c0:c0 + PROJ_COLS] = _dot(xn_s[...], w_in_ref[:, c0:c0 + PROJ_COLS])

    lane = lax.broadcasted_iota(jnp.int32, (1, LANES), 1)
    low = lane < HEAD_DIM
    first_half = (lane & 32) == 0

    for p in range(N_PAIRS):
        for r0 in range(0, ts, NORM_ROWS):
            rs = slice(r0, r0 + NORM_ROWS)
            qc = slice(OFF_QA + p * LANES, OFF_QA + (p + 1) * LANES)
            kc = slice(OFF_KA + p * LANES, OFF_KA + (p + 1) * LANES)
            proj[rs, qc] = _pair_rms(proj[rs, qc], low, qn_ref[...]) * (HEAD_DIM ** -0.5 * LOG2E)
            k_s[ts + r0:ts + r0 + NORM_ROWS, p * LANES:(p + 1) * LANES] = (
                _pair_rms(proj[rs, kc], low, kn_ref[...]).astype(jnp.bfloat16))
    for r0 in range(0, ts, NORM_ROWS):
        v_s[ts + r0:ts + r0 + NORM_ROWS, :] = (
            proj[r0:r0 + NORM_ROWS, OFF_VA:OFF_VA + D_ATT].astype(jnp.bfloat16))

    col = lax.broadcasted_iota(jnp.int32, (1, BAND), 1)
    for c2 in range(ts // Q_BLOCK):
        r0 = c2 * Q_BLOCK
        first_valid = jnp.where(t == 0, ts - r0, 0)
        key_ok = col >= first_valid
        for p in range(N_PAIRS):
            sl = slice(p * LANES, (p + 1) * LANES)
            q = proj[r0:r0 + Q_BLOCK, OFF_QA + p * LANES:OFF_QA + (p + 1) * LANES]
            q2 = jnp.concatenate([jnp.where(low, q, 0.0), jnp.where(low, 0.0, q)],
                                 axis=0).astype(jnp.bfloat16)
            kb = k_s[r0:r0 + BAND, sl]
            vb = v_s[r0:r0 + BAND, sl]
            s = lax.dot_general(q2, kb, _NT, preferred_element_type=jnp.float32)
            s = jnp.where(key_ok, s + bias_ref[p], NEG)
            m = jnp.max(s, axis=-1, keepdims=True)
            e = jnp.exp2(s - m)
            den = jnp.sum(e, axis=-1, keepdims=True)
            o2 = _dot(e.astype(jnp.bfloat16), vb) / den
            o = jnp.where(low, o2[0:Q_BLOCK], o2[Q_BLOCK:2 * Q_BLOCK])
            mix_s[r0:r0 + Q_BLOCK, MIX_ATT + p * LANES:MIX_ATT + (p + 1) * LANES] = (
                o.astype(jnp.bfloat16))
    k_s[0:ts, :] = k_s[ts:2 * ts, :]
    v_s[0:ts, :] = v_s[ts:2 * ts, :]

    a = proj[:, OFF_CA:OFF_CA + D_CONV]
    gate = proj[:, OFF_CG:OFF_CG + D_CONV]
    glu_s[CONV_HIST:CONV_HIST + ts, :] = a * jax.nn.sigmoid(gate)
    base = CONV_HIST - (CONV_K - 1)

    def conv_block(r):
        acc = None
        for rho in range(SUBLANES):
            rows = CONV_ROWS + (SUBLANES if rho else 0)
            z = None
            for kk in range(CONV_K):
                if (base + kk) % SUBLANES != rho:
                    continue
                off = r + base + kk - rho
                term = cw_ref[kk:kk + 1, :] * glu_s[off:off + rows, :]
                z = term if z is None else z + term
            z = z[rho:rho + CONV_ROWS]
            acc = z if acc is None else acc + z
        y = acc + cb_ref[...]
        mu = jnp.mean(y, axis=-1, keepdims=True)
        yc = y - mu
        var = jnp.mean(yc * yc, axis=-1, keepdims=True)
        y = (yc * lax.rsqrt(var + EPS)) * lng_ref[...] + lnb_ref[...]
        y = y * jax.nn.sigmoid(y)
        y = _dot(y.astype(jnp.bfloat16), pww_ref[...]) + pwb_ref[...]
        mix_s[r:r + CONV_ROWS, MIX_CONV:MIX_CONV + D_CONV] = y.astype(jnp.bfloat16)

    row2 = lax.broadcasted_iota(jnp.int32, (2 * CHUNK, LANES), 0)
    lane2 = lax.broadcasted_iota(jnp.int32, (2 * CHUNK, LANES), 1)
    own2 = (row2 < CHUNK) == (lane2 < HEAD_DIM)
    for n in range(ts // CHUNK):
        rs = slice(n * CHUNK, (n + 1) * CHUNK)
        conv_block(n * CHUNK)
        cos = cos_ref[rs, :]
        sin = sin_ref[rs, :]
        for p in range(N_PAIRS):
            q = proj[rs, OFF_QR + p * LANES:OFF_QR + (p + 1) * LANES]
            k = proj[rs, OFF_KR + p * LANES:OFF_KR + (p + 1) * LANES]
            v = proj[rs, OFF_VR + p * LANES:OFF_VR + (p + 1) * LANES]
            g = proj[rs, OFF_GR + p * LANES:OFF_GR + (p + 1) * LANES]
            qr = (q * cos + _swap_halves(q, first_half) * sin) * (HEAD_DIM ** -0.5)
            kr = k * cos + _swap_halves(k, first_half) * sin
            k2 = jnp.where(own2, jnp.concatenate([kr, kr], axis=0), 0.0).astype(jnp.bfloat16)
            v2 = jnp.where(own2, jnp.concatenate([v, v], axis=0), 0.0).astype(jnp.bfloat16)
            s = lax.dot_general(qr.astype(jnp.bfloat16), k2, _NT,
                                preferred_element_type=jnp.float32)
            sd = (s * dmat_ref[p]).astype(jnp.bfloat16)
            qd = (qr * qdec_ref[p]).astype(jnp.bfloat16)
            state = st_s[p]
            lhs = jnp.concatenate([qd, sd], axis=1)
            rhs = jnp.concatenate([state.astype(jnp.bfloat16), v2], axis=0)
            o = _dot(lhs, rhs)
            cen = o - _pair_sum(o, low) * (1.0 / HEAD_DIM)
            var = _pair_sum(cen * cen, low) * (1.0 / HEAD_DIM)
            y = (cen * lax.rsqrt(var + EPS)) * gn_ref[p]
            y = (g * jax.nn.sigmoid(g)) * y
            mix_s[rs, MIX_RET + p * LANES:MIX_RET + (p + 1) * LANES] = y.astype(jnp.bfloat16)
            kd = (kr * kdec_ref[p]).astype(jnp.bfloat16)
            kv = lax.dot_general(kd, v.astype(jnp.bfloat16), _TN,
                                 preferred_element_type=jnp.float32)
            st_s[p] = cdec_ref[p] * state + jnp.where(own2, kv, 0.0)
    glu_s[0:CONV_HIST, :] = glu_s[ts:ts + CONV_HIST, :]

    o_ref[...] = hc_ref[...] + _dot(mix_s[...], w_o_ref[...])


def _ffn_kernel(h_ref, p_ref, gf_ref, w1_ref, w2_ref, gp_ref, wpg_ref, wple_ref, o_ref):
    x = h_ref[...]
    hn = _rms(x, gf_ref[...]).astype(jnp.bfloat16)
    acc = jnp.zeros_like(x)
    for j in range(0, D_FF, FF_BLOCK):
        u = jnp.maximum(_dot(hn, w1_ref[:, j:j + FF_BLOCK]), 0.0)
        acc = acc + _dot((u * u).astype(jnp.bfloat16), w2_ref[j:j + FF_BLOCK, :])
    h2 = x + acc
    gn = _rms(h2, gp_ref[...]).astype(jnp.bfloat16)
    gate = jax.nn.sigmoid(_dot(gn, wpg_ref[...]))
    ple = _dot(p_ref[...].astype(jnp.bfloat16), wple_ref[...])
    o_ref[...] = h2 + gate * ple


def _const_spec(shape):
    nd = len(shape)
    return pl.BlockSpec(shape, lambda *_: (0,) * nd, pipeline_mode=pl.Buffered(1))


def _layer_spec(shape, layer):
    nd = len(shape)
    return pl.BlockSpec((None,) + tuple(shape[1:]), lambda *_: (layer,) + (0,) * (nd - 1),
                        pipeline_mode=pl.Buffered(1))


def _mixer_layer(h, layer, prm, consts):
    b, s, _ = h.shape
    ts = SEQ_TILE
    nt = s // ts
    n_steps = b * nt
    h3 = h.reshape(n_steps, ts, D_MODEL)
    stacked = [prm[k] for k in ("g_mix", "w_in", "qn_g", "kn_g", "bias", "conv_w", "conv_b",
                                "ln_g", "ln_b", "pw_w", "pw_b")]
    tables = [consts[k] for k in ("dmat", "qdec", "kdec", "cdec")]
    cur_spec = pl.BlockSpec((None, ts, D_MODEL), lambda i: (i, 0, 0))
    rope_spec = pl.BlockSpec((ts, LANES), lambda i: (lax.rem(i, nt), 0))
    in_specs = ([cur_spec] + [_layer_spec(a.shape, layer) for a in stacked]
                + [rope_spec, rope_spec] + [_const_spec(a.shape) for a in tables]
                + [_layer_spec(prm["gn_g"].shape, layer), _layer_spec(prm["w_o"].shape, layer)])
    out = pl.pallas_call(
        partial(_mixer_kernel, tiles_per_seq=nt),
        grid=(n_steps,),
        in_specs=in_specs,
        out_specs=cur_spec,
        out_shape=jax.ShapeDtypeStruct(h3.shape, jnp.float32),
        scratch_shapes=[
            pltpu.VMEM((ts, D_MODEL), jnp.bfloat16),
            pltpu.VMEM((ts, D_IN), jnp.float32),
            pltpu.VMEM((2 * ts, D_ATT), jnp.bfloat16),
            pltpu.VMEM((2 * ts, D_ATT), jnp.bfloat16),
            pltpu.VMEM((CONV_HIST + ts, D_CONV), jnp.float32),
            pltpu.VMEM((N_PAIRS, LANES, LANES), jnp.float32),
            pltpu.VMEM((ts, D_MODEL), jnp.bfloat16),
        ],
        compiler_params=pltpu.CompilerParams(
            dimension_semantics=("arbitrary",),
            vmem_limit_bytes=VMEM_LIMIT_BYTES),
        name="mixer_layer",
    )(h3, *stacked, consts["cos"], consts["sin"], *tables, prm["gn_g"], prm["w_o"])
    return out.reshape(b, s, D_MODEL)


def _ffn_layer(h, layer, prm):
    b, s, _ = h.shape
    t = b * s
    stacked = [prm[k] for k in ("g_ffn", "w1", "w2", "g_ple", "w_pg", "w_ple")]
    in_specs = ([pl.BlockSpec((FFN_TILE, D_MODEL), lambda i: (i, 0)),
                 pl.BlockSpec((None, FFN_TILE, D_PLE), lambda i: (layer, i, 0))]
                + [_layer_spec(a.shape, layer) for a in stacked])
    out = pl.pallas_call(
        _ffn_kernel,
        grid=(t // FFN_TILE,),
        in_specs=in_specs,
        out_specs=pl.BlockSpec((FFN_TILE, D_MODEL), lambda i: (i, 0)),
        out_shape=jax.ShapeDtypeStruct((t, D_MODEL), jnp.float32),
        compiler_params=pltpu.CompilerParams(
            dimension_semantics=("arbitrary",),
            vmem_limit_bytes=VMEM_LIMIT_BYTES),
        name="ffn_ple_layer",
    )(h.reshape(t, D_MODEL), prm["p"], *stacked)
    return out.reshape(b, s, D_MODEL)


def _pair_lanes(per_head):
    x = np.repeat(per_head[..., None], HEAD_DIM, axis=-1)
    x = x.reshape((N_PAIRS, 2) + x.shape[1:])
    return np.concatenate([x[:, 0], x[:, 1]], axis=-1)


def _retention_consts(s):
    pos = np.arange(s, dtype=np.float64)
    inv_freq = ROPE_THETA ** (-np.arange(0, HEAD_DIM, 2, dtype=np.float64) / HEAD_DIM)
    ang = pos[:, None] * inv_freq[None, :]
    cos, sin = np.cos(ang), np.sin(ang)
    cos4 = np.concatenate([cos, cos, cos, cos], axis=-1)
    sin4 = np.concatenate([-sin, sin, -sin, sin], axis=-1)
    log_gamma = np.log(1.0 - 2.0 ** (-5.0 - np.arange(H_RET, dtype=np.float64)))
    n = np.arange(CHUNK, dtype=np.float64)
    dist = np.abs(n[:, None] - n[None, :])
    d_intra = np.exp(log_gamma[:, None, None] * dist[None])
    d_pair = d_intra.reshape(N_PAIRS, 2, CHUNK, CHUNK)
    dmat = np.concatenate([d_pair[:, 0], d_pair[:, 1]], axis=-1)
    q_dec = np.exp(log_gamma[:, None] * (n + 1.0)[None, :])
    k_dec = np.exp(log_gamma[:, None] * (CHUNK - 1 - n)[None, :])
    chunk_decay = np.exp(log_gamma * CHUNK).reshape(N_PAIRS, 2)
    qdec = _pair_lanes(q_dec)
    kdec = _pair_lanes(k_dec)
    ones = np.ones((HEAD_DIM, HEAD_DIM))
    cdec = np.stack([np.kron(np.diag(cd), ones) for cd in chunk_decay])
    tabs = dict(cos=cos4, sin=sin4, dmat=dmat, qdec=qdec, kdec=kdec, cdec=cdec)
    return {k: jnp.asarray(np.ascontiguousarray(v), dtype=jnp.float32) for k, v in tabs.items()}


def _attention_bias(rel_bias):
    lead = rel_bias.shape[:-1]
    own = (N_LEFT_CHUNKS + 1) * CHUNK
    n_m = own + CHUNK - 1
    n_far = N_LEFT_CHUNKS * CHUNK + CHUNK - REL_CLIP
    n_near = n_m - n_far
    rb = rel_bias.astype(jnp.float32) * LOG2E
    far = jnp.broadcast_to(rb[..., 2 * REL_CLIP:], lead + (n_far,))
    near = rb[..., 2 * REL_CLIP - n_near:2 * REL_CLIP][..., ::-1]
    vec = jnp.concatenate([far, near, jnp.zeros(lead + (1,), jnp.float32)], axis=-1)
    skew = jnp.tile(vec, CHUNK)[..., :CHUNK * n_m].reshape(lead + (CHUNK, n_m))
    band = skew[..., CHUNK - 1:CHUNK - 1 + own]
    neg = jnp.full(lead + (CHUNK, BAND - own), NEG, jnp.float32)
    per_head = jnp.concatenate([jnp.concatenate([band, neg], axis=-1),
                                jnp.concatenate([neg, band], axis=-1)], axis=-2)
    return per_head.reshape(lead[0], N_PAIRS, 2 * Q_BLOCK, BAND)


def kernel(x, p, norm_mix_g, w_in, qn_g, kn_g, rel_bias, conv_w, conv_b, conv_ln_g,
           conv_ln_b, conv_pw_w, conv_pw_b, ret_gn_g, w_o, norm_ffn_g, w1, w2,
           norm_ple_g, w_pg, w_ple):
    depth = w_in.shape[0]
    b, s, _ = x.shape
    assert s % SEQ_TILE == 0 and (b * s) % FFN_TILE == 0
    bf16 = jnp.bfloat16
    row = lambda a: a[:, None, :]
    prm = dict(
        g_mix=row(norm_mix_g), w_in=w_in.astype(bf16),
        qn_g=row(jnp.tile(qn_g, (1, 2))), kn_g=row(jnp.tile(kn_g, (1, 2))),
        bias=_attention_bias(rel_bias),
        conv_w=jnp.pad(conv_w, ((0, 0), (0, 1), (0, 0))), conv_b=row(conv_b),
        ln_g=row(conv_ln_g), ln_b=row(conv_ln_b),
        pw_w=conv_pw_w.astype(bf16), pw_b=row(conv_pw_b),
        gn_g=ret_gn_g.reshape(depth, N_PAIRS, 1, LANES), w_o=w_o.astype(bf16),
        g_ffn=row(norm_ffn_g), w1=w1.astype(bf16), w2=w2.astype(bf16),
        g_ple=row(norm_ple_g), w_pg=w_pg.astype(bf16), w_ple=w_ple.astype(bf16),
        p=p.reshape(depth, b * s, D_PLE),
    )
    consts = _retention_consts(s)
    h = x
    for layer in range(depth):
        h = _mixer_layer(h, layer, prm, consts)
        h = _ffn_layer(h, layer, prm)
    return h
```

```python
import math
from functools import partial

import numpy as np
import jax
import jax.numpy as jnp
from jax import lax
from jax.experimental import pallas as pl
from jax.experimental.pallas import tpu as pltpu

D_MODEL = 1024
CHUNK = 64
N_LEFT_CHUNKS = 8
HEAD_DIM = 64
D_ATT = 384
D_RET = 384
D_CONV = 256
H_ATT = D_ATT // HEAD_DIM
H_RET = D_RET // HEAD_DIM
CONV_K = 31
REL_CLIP = 128
D_FF = 4 * D_MODEL
D_PLE = 256
ROPE_THETA = 10000.0
EPS = 1e-6
D_IN = 3 * D_ATT + 2 * D_CONV + 4 * D_RET

OFF_QA, OFF_KA, OFF_VA = 0, D_ATT, 2 * D_ATT
OFF_CA = 3 * D_ATT
OFF_CG = OFF_CA + D_CONV
OFF_QR = OFF_CA + 2 * D_CONV
OFF_KR = OFF_QR + D_RET
OFF_VR = OFF_KR + D_RET
OFF_GR = OFF_VR + D_RET
MIX_ATT, MIX_CONV, MIX_RET = 0, D_ATT, D_ATT + D_CONV

LANES = 128
SUBLANES = 8
N_PAIRS = D_ATT // LANES
SEQ_TILE = N_LEFT_CHUNKS * CHUNK
Q_BLOCK = 2 * CHUNK
BAND = (N_LEFT_CHUNKS + 2) * CHUNK
CONV_HIST = 32
CONV_ROWS = 64
NORM_ROWS = 64
ATT_PROJ_COLS = 384
LATE_PROJ_COLS = 256
GLU_AFTER_ATT_BLOCKS = 2
RET_AFTER_ATT_BLOCKS = 8
RET_STEPS_PER_ATT_BLOCK = 2
CONV_ROWS_IN_ATTENTION = 512
FFN_TILE = 512
FF_BLOCK = 512
NEG = -1e30
LOG2E = math.log2(math.e)
VMEM_LIMIT_BYTES = 56 * 1024 * 1024

_NT = (((1,), (1,)), ((), ()))
_TN = (((0,), (0,)), ((), ()))


def _rms(x, g):
    return (x * lax.rsqrt(jnp.mean(x * x, axis=-1, keepdims=True) + EPS)) * g


def _dot(a, b):
    return jnp.dot(a, b, preferred_element_type=jnp.float32)


def _pair_sum(x, low):
    s0 = jnp.sum(jnp.where(low, x, 0.0), axis=-1, keepdims=True)
    s1 = jnp.sum(jnp.where(low, 0.0, x), axis=-1, keepdims=True)
    return jnp.where(low, s0, s1)


def _pair_rms(x, low, g):
    ms = _pair_sum(x * x, low) * (1.0 / HEAD_DIM)
    return (x * lax.rsqrt(ms + EPS)) * g


def _swap_halves(x, first_half):
    return jnp.where(first_half, pltpu.roll(x, 96, 1), pltpu.roll(x, 32, 1))


def _mixer_kernel(hc_ref, g_ref, w_in_ref, qn_ref, kn_ref, bias_ref,
                  cw_ref, cb_ref, lng_ref, lnb_ref, pww_ref, pwb_ref,
                  cos_ref, sin_ref, dmat_ref, qdec_ref, kdec_ref, cdec_ref, gn_ref,
                  w_o_ref, o_ref, xn_s, proj, k_s, v_s, glu_s, st_s, mix_s, *, tiles_per_seq):
    ts = SEQ_TILE
    t = lax.rem(pl.program_id(0), tiles_per_seq)

    @pl.when(t == 0)
    def _():
        k_s[0:ts, :] = jnp.zeros((ts, D_ATT), jnp.bfloat16)
        v_s[0:ts, :] = jnp.zeros((ts, D_ATT), jnp.bfloat16)
        glu_s[0:CONV_HIST, :] = jnp.zeros((CONV_HIST, D_CONV), jnp.float32)
        st_s[...] = jnp.zeros_like(st_s)

    xn_s[...] = _rms(hc_ref[...], g_ref[...]).astype(jnp.bfloat16)
    for c0 in range(0, OFF_CA, ATT_PROJ_COLS):
        proj[:, c0:c0 + ATT_PROJ_COLS] = _dot(xn_s[...], w_in_ref[:, c0:c0 + ATT_PROJ_COLS])
    pieces = list(range(OFF_CA, D_IN, LATE_PROJ_COLS))

    def late_projection(n):
        for _ in range(n):
            if pieces:
                c0 = pieces.pop(0)
                proj[:, c0:c0 + LATE_PROJ_COLS] = _dot(xn_s[...],
                                                       w_in_ref[:, c0:c0 + LATE_PROJ_COLS])

    lane = lax.broadcasted_iota(jnp.int32, (1, LANES), 1)
    low = lane < HEAD_DIM
    first_half = (lane & 32) == 0

    def fill_glu():
        a = proj[:, OFF_CA:OFF_CA + D_CONV]
        gate = proj[:, OFF_CG:OFF_CG + D_CONV]
        glu_s[CONV_HIST:CONV_HIST + ts, :] = a * jax.nn.sigmoid(gate)

    base = CONV_HIST - (CONV_K - 1)

    def conv_block(r):
        acc = None
        for rho in range(SUBLANES):
            rows = CONV_ROWS + (SUBLANES if rho else 0)
            z = None
            for kk in range(CONV_K):
                if (base + kk) % SUBLANES != rho:
                    continue
                off = r + base + kk - rho
                term = cw_ref[kk:kk + 1, :] * glu_s[off:off + rows, :]
                z = term if z is None else z + term
            z = z[rho:rho + CONV_ROWS]
            acc = z if acc is None else acc + z
        y = acc + cb_ref[...]
        mu = jnp.mean(y, axis=-1, keepdims=True)
        yc = y - mu
        var = jnp.mean(yc * yc, axis=-1, keepdims=True)
        y = (yc * lax.rsqrt(var + EPS)) * lng_ref[...] + lnb_ref[...]
        y = y * jax.nn.sigmoid(y)
        y = _dot(y.astype(jnp.bfloat16), pww_ref[...]) + pwb_ref[...]
        mix_s[r:r + CONV_ROWS, MIX_CONV:MIX_CONV + D_CONV] = y.astype(jnp.bfloat16)

    row2 = lax.broadcasted_iota(jnp.int32, (2 * CHUNK, LANES), 0)
    lane2 = lax.broadcasted_iota(jnp.int32, (2 * CHUNK, LANES), 1)
    own2 = (row2 < CHUNK) == (lane2 < HEAD_DIM)

    def retention_step(n):
        rs = slice(n * CHUNK, (n + 1) * CHUNK)
        cos = cos_ref[rs, :]
        sin = sin_ref[rs, :]
        for p in range(N_PAIRS):
            q = proj[rs, OFF_QR + p * LANES:OFF_QR + (p + 1) * LANES]
            k = proj[rs, OFF_KR + p * LANES:OFF_KR + (p + 1) * LANES]
            v = proj[rs, OFF_VR + p * LANES:OFF_VR + (p + 1) * LANES]
            g = proj[rs, OFF_GR + p * LANES:OFF_GR + (p + 1) * LANES]
            qr = (q * cos + _swap_halves(q, first_half) * sin) * (HEAD_DIM ** -0.5)
            kr = k * cos + _swap_halves(k, first_half) * sin
            k2 = jnp.where(own2, jnp.concatenate([kr, kr], axis=0), 0.0).astype(jnp.bfloat16)
            v2 = jnp.where(own2, jnp.concatenate([v, v], axis=0), 0.0).astype(jnp.bfloat16)
            s = lax.dot_general(qr.astype(jnp.bfloat16), k2, _NT,
                                preferred_element_type=jnp.float32)
            sd = (s * dmat_ref[p]).astype(jnp.bfloat16)
            qd = (qr * qdec_ref[p]).astype(jnp.bfloat16)
            state = st_s[p]
            lhs = jnp.concatenate([qd, sd], axis=1)
            rhs = jnp.concatenate([state.astype(jnp.bfloat16), v2], axis=0)
            o = _dot(lhs, rhs)
            cen = o - _pair_sum(o, low) * (1.0 / HEAD_DIM)
            var = _pair_sum(cen * cen, low) * (1.0 / HEAD_DIM)
            y = (cen * lax.rsqrt(var + EPS)) * gn_ref[p]
            y = (g * jax.nn.sigmoid(g)) * y
            mix_s[rs, MIX_RET + p * LANES:MIX_RET + (p + 1) * LANES] = y.astype(jnp.bfloat16)
            kd = (kr * kdec_ref[p]).astype(jnp.bfloat16)
            kv = lax.dot_general(kd, v.astype(jnp.bfloat16), _TN,
                                 preferred_element_type=jnp.float32)
            st_s[p] = cdec_ref[p] * state + jnp.where(own2, kv, 0.0)

    for p in range(N_PAIRS):
        for r0 in range(0, ts, NORM_ROWS):
            rs = slice(r0, r0 + NORM_ROWS)
            qc = slice(OFF_QA + p * LANES, OFF_QA + (p + 1) * LANES)
            kc = slice(OFF_KA + p * LANES, OFF_KA + (p + 1) * LANES)
            proj[rs, qc] = _pair_rms(proj[rs, qc], low, qn_ref[...]) * (HEAD_DIM ** -0.5 * LOG2E)
            k_s[ts + r0:ts + r0 + NORM_ROWS, p * LANES:(p + 1) * LANES] = (
                _pair_rms(proj[rs, kc], low, kn_ref[...]).astype(jnp.bfloat16))
    for r0 in range(0, ts, NORM_ROWS):
        v_s[ts + r0:ts + r0 + NORM_ROWS, :] = (
            proj[r0:r0 + NORM_ROWS, OFF_VA:OFF_VA + D_ATT].astype(jnp.bfloat16))

    col = lax.broadcasted_iota(jnp.int32, (1, BAND), 1)
    n_att_blocks = 0
    conv_rows_done = 0
    ret_steps_done = 0
    for c2 in range(ts // Q_BLOCK):
        r0 = c2 * Q_BLOCK
        first_valid = jnp.where(t == 0, ts - r0, 0)
        key_ok = col >= first_valid
        for p in range(N_PAIRS):
            sl = slice(p * LANES, (p + 1) * LANES)
            q = proj[r0:r0 + Q_BLOCK, OFF_QA + p * LANES:OFF_QA + (p + 1) * LANES]
            q2 = jnp.concatenate([jnp.where(low, q, 0.0), jnp.where(low, 0.0, q)],
                                 axis=0).astype(jnp.bfloat16)
            kb = k_s[r0:r0 + BAND, sl]
            vb = v_s[r0:r0 + BAND, sl]
            s = lax.dot_general(q2, kb, _NT, preferred_element_type=jnp.float32)
            s = jnp.where(key_ok, s + bias_ref[p], NEG)
            m = jnp.max(s, axis=-1, keepdims=True)
            e = jnp.exp2(s - m)
            den = jnp.sum(e, axis=-1, keepdims=True)
            o2 = _dot(e.astype(jnp.bfloat16), vb) / den
            o = jnp.where(low, o2[0:Q_BLOCK], o2[Q_BLOCK:2 * Q_BLOCK])
            mix_s[r0:r0 + Q_BLOCK, MIX_ATT + p * LANES:MIX_ATT + (p + 1) * LANES] = (
                o.astype(jnp.bfloat16))
            n_att_blocks += 1
            late_projection(1)
            if n_att_blocks == GLU_AFTER_ATT_BLOCKS:
                fill_glu()
            if n_att_blocks > GLU_AFTER_ATT_BLOCKS and conv_rows_done < CONV_ROWS_IN_ATTENTION:
                conv_block(conv_rows_done)
                conv_rows_done += CONV_ROWS
            if not pieces:
                for _ in range(RET_STEPS_PER_ATT_BLOCK):
                    if n_att_blocks > RET_AFTER_ATT_BLOCKS and ret_steps_done < ts // CHUNK:
                        retention_step(ret_steps_done)
                        ret_steps_done += 1
    late_projection(len(pieces))
    k_s[0:ts, :] = k_s[ts:2 * ts, :]
    v_s[0:ts, :] = v_s[ts:2 * ts, :]

    while conv_rows_done < ts:
        conv_block(conv_rows_done)
        conv_rows_done += CONV_ROWS
    while ret_steps_done < ts // CHUNK:
        retention_step(ret_steps_done)
        ret_steps_done += 1
    glu_s[0:CONV_HIST, :] = glu_s[ts:ts + CONV_HIST, :]

    o_ref[...] = hc_ref[...] + _dot(mix_s[...], w_o_ref[...])


def _ffn_kernel(h_ref, p_ref, gf_ref, w1_ref, w2_ref, gp_ref, wpg_ref, wple_ref, o_ref):
    x = h_ref[...]
    hn = _rms(x, gf_ref[...]).astype(jnp.bfloat16)
    acc = jnp.zeros_like(x)
    for j in range(0, D_FF, FF_BLOCK):
        u = jnp.maximum(_dot(hn, w1_ref[:, j:j + FF_BLOCK]), 0.0)
        acc = acc + _dot((u * u).astype(jnp.bfloat16), w2_ref[j:j + FF_BLOCK, :])
    h2 = x + acc
    gn = _rms(h2, gp_ref[...]).astype(jnp.bfloat16)
    gate = jax.nn.sigmoid(_dot(gn, wpg_ref[...]))
    ple = _dot(p_ref[...].astype(jnp.bfloat16), wple_ref[...])
    o_ref[...] = h2 + gate * ple


def _const_spec(shape):
    nd = len(shape)
    return pl.BlockSpec(shape, lambda *_: (0,) * nd, pipeline_mode=pl.Buffered(1))


def _layer_spec(shape, layer):
    nd = len(shape)
    return pl.BlockSpec((None,) + tuple(shape[1:]), lambda *_: (layer,) + (0,) * (nd - 1),
                        pipeline_mode=pl.Buffered(1))


def _mixer_layer(h, layer, prm, consts):
    b, s, _ = h.shape
    ts = SEQ_TILE
    nt = s // ts
    n_steps = b * nt
    h3 = h.reshape(n_steps, ts, D_MODEL)
    stacked = [prm[k] for k in ("g_mix", "w_in", "qn_g", "kn_g", "bias", "conv_w", "conv_b",
                                "ln_g", "ln_b", "pw_w", "pw_b")]
    tables = [consts[k] for k in ("dmat", "qdec", "kdec", "cdec")]
    cur_spec = pl.BlockSpec((None, ts, D_MODEL), lambda i: (i, 0, 0))
    rope_spec = pl.BlockSpec((ts, LANES), lambda i: (lax.rem(i, nt), 0))
    in_specs = ([cur_spec] + [_layer_spec(a.shape, layer) for a in stacked]
                + [rope_spec, rope_spec] + [_const_spec(a.shape) for a in tables]
                + [_layer_spec(prm["gn_g"].shape, layer), _layer_spec(prm["w_o"].shape, layer)])
    out = pl.pallas_call(
        partial(_mixer_kernel, tiles_per_seq=nt),
        grid=(n_steps,),
        in_specs=in_specs,
        out_specs=cur_spec,
        out_shape=jax.ShapeDtypeStruct(h3.shape, jnp.float32),
        scratch_shapes=[
            pltpu.VMEM((ts, D_MODEL), jnp.bfloat16),
            pltpu.VMEM((ts, D_IN), jnp.float32),
            pltpu.VMEM((2 * ts, D_ATT), jnp.bfloat16),
            pltpu.VMEM((2 * ts, D_ATT), jnp.bfloat16),
            pltpu.VMEM((CONV_HIST + ts, D_CONV), jnp.float32),
            pltpu.VMEM((N_PAIRS, LANES, LANES), jnp.float32),
            pltpu.VMEM((ts, D_MODEL), jnp.bfloat16),
        ],
        compiler_params=pltpu.CompilerParams(
            dimension_semantics=("arbitrary",),
            vmem_limit_bytes=VMEM_LIMIT_BYTES),
        name="mixer_layer",
    )(h3, *stacked, consts["cos"], consts["sin"], *tables, prm["gn_g"], prm["w_o"])
    return out.reshape(b, s, D_MODEL)


def _ffn_layer(h, layer, prm):
    b, s, _ = h.shape
    t = b * s
    stacked = [prm[k] for k in ("g_ffn", "w1", "w2", "g_ple", "w_pg", "w_ple")]
    in_specs = ([pl.BlockSpec((FFN_TILE, D_MODEL), lambda i: (i, 0)),
                 pl.BlockSpec((None, FFN_TILE, D_PLE), lambda i: (layer, i, 0))]
                + [_layer_spec(a.shape, layer) for a in stacked])
    out = pl.pallas_call(
        _ffn_kernel,
        grid=(t // FFN_TILE,),
        in_specs=in_specs,
        out_specs=pl.BlockSpec((FFN_TILE, D_MODEL), lambda i: (i, 0)),
        out_shape=jax.ShapeDtypeStruct((t, D_MODEL), jnp.float32),
        compiler_params=pltpu.CompilerParams(
            dimension_semantics=("arbitrary",),
            vmem_limit_bytes=VMEM_LIMIT_BYTES),
        name="ffn_ple_layer",
    )(h.reshape(t, D_MODEL), prm["p"], *stacked)
    return out.reshape(b, s, D_MODEL)


def _pair_lanes(per_head):
    x = np.repeat(per_head[..., None], HEAD_DIM, axis=-1)
    x = x.reshape((N_PAIRS, 2) + x.shape[1:])
    return np.concatenate([x[:, 0], x[:, 1]], axis=-1)


def _retention_consts(s):
    pos = np.arange(s, dtype=np.float64)
    inv_freq = ROPE_THETA ** (-np.arange(0, HEAD_DIM, 2, dtype=np.float64) / HEAD_DIM)
    ang = pos[:, None] * inv_freq[None, :]
    cos, sin = np.cos(ang), np.sin(ang)
    cos4 = np.concatenate([cos, cos, cos, cos], axis=-1)
    sin4 = np.concatenate([-sin, sin, -sin, sin], axis=-1)
    log_gamma = np.log(1.0 - 2.0 ** (-5.0 - np.arange(H_RET, dtype=np.float64)))
    n = np.arange(CHUNK, dtype=np.float64)
    dist = np.abs(n[:, None] - n[None, :])
    d_intra = np.exp(log_gamma[:, None, None] * dist[None])
    d_pair = d_intra.reshape(N_PAIRS, 2, CHUNK, CHUNK)
    dmat = np.concatenate([d_pair[:, 0], d_pair[:, 1]], axis=-1)
    q_dec = np.exp(log_gamma[:, None] * (n + 1.0)[None, :])
    k_dec = np.exp(log_gamma[:, None] * (CHUNK - 1 - n)[None, :])
    chunk_decay = np.exp(log_gamma * CHUNK).reshape(N_PAIRS, 2)
    qdec = _pair_lanes(q_dec)
    kdec = _pair_lanes(k_dec)
    ones = np.ones((HEAD_DIM, HEAD_DIM))
    cdec = np.stack([np.kron(np.diag(cd), ones) for cd in chunk_decay])
    tabs = dict(cos=cos4, sin=sin4, dmat=dmat, qdec=qdec, kdec=kdec, cdec=cdec)
    return {k: jnp.asarray(np.ascontiguousarray(v), dtype=jnp.float32) for k, v in tabs.items()}


def _attention_bias(rel_bias):
    lead = rel_bias.shape[:-1]
    own = (N_LEFT_CHUNKS + 1) * CHUNK
    n_m = own + CHUNK - 1
    n_far = N_LEFT_CHUNKS * CHUNK + CHUNK - REL_CLIP
    n_near = n_m - n_far
    rb = rel_bias.astype(jnp.float32) * LOG2E
    far = jnp.broadcast_to(rb[..., 2 * REL_CLIP:], lead + (n_far,))
    near = rb[..., 2 * REL_CLIP - n_near:2 * REL_CLIP][..., ::-1]
    vec = jnp.concatenate([far, near, jnp.zeros(lead + (1,), jnp.float32)], axis=-1)
    skew = jnp.tile(vec, CHUNK)[..., :CHUNK * n_m].reshape(lead + (CHUNK, n_m))
    band = skew[..., CHUNK - 1:CHUNK - 1 + own]
    neg = jnp.full(lead + (CHUNK, BAND - own), NEG, jnp.float32)
    per_head = jnp.concatenate([jnp.concatenate([band, neg], axis=-1),
                                jnp.concatenate([neg, band], axis=-1)], axis=-2)
    return per_head.reshape(lead[0], N_PAIRS, 2 * Q_BLOCK, BAND)


def kernel(x, p, norm_mix_g, w_in, qn_g, kn_g, rel_bias, conv_w, conv_b, conv_ln_g,
           conv_ln_b, conv_pw_w, conv_pw_b, ret_gn_g, w_o, norm_ffn_g, w1, w2,
           norm_ple_g, w_pg, w_ple):
    depth = w_in.shape[0]
    b, s, _ = x.shape
    assert s % SEQ_TILE == 0 and (b * s) % FFN_TILE == 0
    bf16 = jnp.bfloat16
    row = lambda a: a[:, None, :]
    prm = dict(
        g_mix=row(norm_mix_g), w_in=w_in.astype(bf16),
        qn_g=row(jnp.tile(qn_g, (1, 2))), kn_g=row(jnp.tile(kn_g, (1, 2))),
        bias=_attention_bias(rel_bias),
        conv_w=jnp.pad(conv_w, ((0, 0), (0, 1), (0, 0))), conv_b=row(conv_b),
        ln_g=row(conv_ln_g), ln_b=row(conv_ln_b),
        pw_w=conv_pw_w.astype(bf16), pw_b=row(conv_pw_b),
        gn_g=ret_gn_g.reshape(depth, N_PAIRS, 1, LANES), w_o=w_o.astype(bf16),
        g_ffn=row(norm_ffn_g), w1=w1.astype(bf16), w2=w2.astype(bf16),
        g_ple=row(norm_ple_g), w_pg=w_pg.astype(bf16), w_ple=w_ple.astype(bf16),
        p=p.reshape(depth, b * s, D_PLE),
    )
    consts = _retention_consts(s)
    h = x
    for layer in range(depth):
        h = _mixer_layer(h, layer, prm, consts)
        h = _ffn_layer(h, layer, prm)
    return h
```

```python
import math
from functools import partial

import numpy as np
import jax
import jax.numpy as jnp
from jax import lax
from jax.experimental import pallas as pl
from jax.experimental.pallas import tpu as pltpu

D_MODEL = 1024
CHUNK = 64
N_LEFT_CHUNKS = 8
HEAD_DIM = 64
D_ATT = 384
D_RET = 384
D_CONV = 256
H_ATT = D_ATT // HEAD_DIM
H_RET = D_RET // HEAD_DIM
CONV_K = 31
REL_CLIP = 128
D_FF = 4 * D_MODEL
D_PLE = 256
ROPE_THETA = 10000.0
EPS = 1e-6
D_IN = 3 * D_ATT + 2 * D_CONV + 4 * D_RET

OFF_QA, OFF_KA, OFF_VA = 0, D_ATT, 2 * D_ATT
OFF_CA = 3 * D_ATT
OFF_CG = OFF_CA + D_CONV
OFF_QR = OFF_CA + 2 * D_CONV
OFF_KR = OFF_QR + D_RET
OFF_VR = OFF_KR + D_RET
OFF_GR = OFF_VR + D_RET
MIX_ATT, MIX_CONV, MIX_RET = 0, D_ATT, D_ATT + D_CONV

LANES = 128
SUBLANES = 8
N_PAIRS = D_ATT // LANES
SEQ_TILE = N_LEFT_CHUNKS * CHUNK
Q_BLOCK = 2 * CHUNK
BAND = (N_LEFT_CHUNKS + 2) * CHUNK
CONV_HIST = 32
CONV_ROWS = 64
NORM_ROWS = 64
LATE_PROJ_COLS = 256
RET_STEPS_PER_ATT_BLOCK = 2
OUT_ROWS = SEQ_TILE
CONV_ROWS_IN_ATTENTION = 512
FFN_TILE = 512
FF_BLOCK = 512
NEG = -1e30
LOG2E = math.log2(math.e)
VMEM_LIMIT_BYTES = 56 * 1024 * 1024

_NT = (((1,), (1,)), ((), ()))
_TN = (((0,), (0,)), ((), ()))


def _rms(x, g):
    return (x * lax.rsqrt(jnp.mean(x * x, axis=-1, keepdims=True) + EPS)) * g


def _dot(a, b):
    return jnp.dot(a, b, preferred_element_type=jnp.float32)


def _pair_sum(x, low):
    s0 = jnp.sum(jnp.where(low, x, 0.0), axis=-1, keepdims=True)
    s1 = jnp.sum(jnp.where(low, 0.0, x), axis=-1, keepdims=True)
    return jnp.where(low, s0, s1)


def _pair_rms(x, low, g):
    ms = _pair_sum(x * x, low) * (1.0 / HEAD_DIM)
    return (x * lax.rsqrt(ms + EPS)) * g


def _swap_halves(x, first_half):
    return jnp.where(first_half, pltpu.roll(x, 96, 1), pltpu.roll(x, 32, 1))


def _mixer_kernel(hc_ref, g_ref, w_in_ref, qn_ref, kn_ref, bias_ref,
                  cw_ref, cb_ref, lng_ref, lnb_ref, pww_ref, pwb_ref,
                  cos_ref, sin_ref, dmat_ref, qdec_ref, kdec_ref, cdec_ref, gn_ref,
                  w_o_ref, o_ref, xn_s, proj, k_s, v_s, glu_s, st_s, mix_s, *, tiles_per_seq):
    ts = SEQ_TILE
    t = lax.rem(pl.program_id(0), tiles_per_seq)

    @pl.when(t == 0)
    def _():
        k_s[0:ts, :] = jnp.zeros((ts, D_ATT), jnp.bfloat16)
        v_s[0:ts, :] = jnp.zeros((ts, D_ATT), jnp.bfloat16)
        glu_s[0:CONV_HIST, :] = jnp.zeros((CONV_HIST, D_CONV), jnp.float32)
        st_s[...] = jnp.zeros_like(st_s)

    xn_s[...] = _rms(hc_ref[...], g_ref[...]).astype(jnp.bfloat16)
    pieces = list(range(OFF_CA, D_IN, LATE_PROJ_COLS))

    def late_projection(n):
        for _ in range(n):
            if pieces:
                c0 = pieces.pop(0)
                proj[:, c0:c0 + LATE_PROJ_COLS] = _dot(xn_s[...],
                                                       w_in_ref[:, c0:c0 + LATE_PROJ_COLS])

    lane = lax.broadcasted_iota(jnp.int32, (1, LANES), 1)
    low = lane < HEAD_DIM
    first_half = (lane & 32) == 0

    def fill_glu():
        a = proj[:, OFF_CA:OFF_CA + D_CONV]
        gate = proj[:, OFF_CG:OFF_CG + D_CONV]
        glu_s[CONV_HIST:CONV_HIST + ts, :] = a * jax.nn.sigmoid(gate)

    base = CONV_HIST - (CONV_K - 1)

    def conv_block(r):
        acc = None
        for rho in range(SUBLANES):
            rows = CONV_ROWS + (SUBLANES if rho else 0)
            z = None
            for kk in range(CONV_K):
                if (base + kk) % SUBLANES != rho:
                    continue
                off = r + base + kk - rho
                term = cw_ref[kk:kk + 1, :] * glu_s[off:off + rows, :]
                z = term if z is None else z + term
            z = z[rho:rho + CONV_ROWS]
            acc = z if acc is None else acc + z
        y = acc + cb_ref[...]
        mu = jnp.mean(y, axis=-1, keepdims=True)
        yc = y - mu
        var = jnp.mean(yc * yc, axis=-1, keepdims=True)
        y = (yc * lax.rsqrt(var + EPS)) * lng_ref[...] + lnb_ref[...]
        y = y * jax.nn.sigmoid(y)
        y = _dot(y.astype(jnp.bfloat16), pww_ref[...]) + pwb_ref[...]
        mix_s[r:r + CONV_ROWS, MIX_CONV:MIX_CONV + D_CONV] = y.astype(jnp.bfloat16)

    row2 = lax.broadcasted_iota(jnp.int32, (2 * CHUNK, LANES), 0)
    lane2 = lax.broadcasted_iota(jnp.int32, (2 * CHUNK, LANES), 1)
    own2 = (row2 < CHUNK) == (lane2 < HEAD_DIM)

    def retention_step(n):
        rs = slice(n * CHUNK, (n + 1) * CHUNK)
        cos = cos_ref[rs, :]
        sin = sin_ref[rs, :]
        for p in range(N_PAIRS):
            q = proj[rs, OFF_QR + p * LANES:OFF_QR + (p + 1) * LANES]
            k = proj[rs, OFF_KR + p * LANES:OFF_KR + (p + 1) * LANES]
            v = proj[rs, OFF_VR + p * LANES:OFF_VR + (p + 1) * LANES]
            g = proj[rs, OFF_GR + p * LANES:OFF_GR + (p + 1) * LANES]
            qr = (q * cos + _swap_halves(q, first_half) * sin) * (HEAD_DIM ** -0.5)
            kr = k * cos + _swap_halves(k, first_half) * sin
            k2 = jnp.where(own2, jnp.concatenate([kr, kr], axis=0), 0.0).astype(jnp.bfloat16)
            v2 = jnp.where(own2, jnp.concatenate([v, v], axis=0), 0.0).astype(jnp.bfloat16)
            s = lax.dot_general(qr.astype(jnp.bfloat16), k2, _NT,
                                preferred_element_type=jnp.float32)
            sd = (s * dmat_ref[p]).astype(jnp.bfloat16)
            qd = (qr * qdec_ref[p]).astype(jnp.bfloat16)
            state = st_s[p]
            lhs = jnp.concatenate([qd, sd], axis=1)
            rhs = jnp.concatenate([state.astype(jnp.bfloat16), v2], axis=0)
            o = _dot(lhs, rhs)
            cen = o - _pair_sum(o, low) * (1.0 / HEAD_DIM)
            var = _pair_sum(cen * cen, low) * (1.0 / HEAD_DIM)
            y = (cen * lax.rsqrt(var + EPS)) * gn_ref[p]
            y = (g * jax.nn.sigmoid(g)) * y
            mix_s[rs, MIX_RET + p * LANES:MIX_RET + (p + 1) * LANES] = y.astype(jnp.bfloat16)
            kd = (kr * kdec_ref[p]).astype(jnp.bfloat16)
            kv = lax.dot_general(kd, v.astype(jnp.bfloat16), _TN,
                                 preferred_element_type=jnp.float32)
            st_s[p] = cdec_ref[p] * state + jnp.where(own2, kv, 0.0)

    def att_projection(rb):
        r0 = rb * Q_BLOCK
        rs = slice(r0, r0 + Q_BLOCK)
        proj[rs, 0:OFF_CA] = _dot(xn_s[rs, :], w_in_ref[:, 0:OFF_CA])
        for p in range(N_PAIRS):
            qc = slice(OFF_QA + p * LANES, OFF_QA + (p + 1) * LANES)
            kc = slice(OFF_KA + p * LANES, OFF_KA + (p + 1) * LANES)
            proj[rs, qc] = _pair_rms(proj[rs, qc], low, qn_ref[...]) * (HEAD_DIM ** -0.5 * LOG2E)
            k_s[ts + r0:ts + r0 + Q_BLOCK, p * LANES:(p + 1) * LANES] = (
                _pair_rms(proj[rs, kc], low, kn_ref[...]).astype(jnp.bfloat16))
        v_s[ts + r0:ts + r0 + Q_BLOCK, :] = proj[rs, OFF_VA:OFF_VA + D_ATT].astype(jnp.bfloat16)

    att_projection(0)
    col = lax.broadcasted_iota(jnp.int32, (1, BAND), 1)
    n_att_blocks = 0
    glu_filled = False
    conv_rows_done = 0
    ret_steps_done = 0
    out_rows_done = 0

    def out_projection(r):
        o_ref[r:r + OUT_ROWS, :] = hc_ref[r:r + OUT_ROWS, :] + _dot(mix_s[r:r + OUT_ROWS, :],
                                                                     w_o_ref[...])
    for c2 in range(ts // Q_BLOCK):
        r0 = c2 * Q_BLOCK
        first_valid = jnp.where(t == 0, ts - r0, 0)
        key_ok = col >= first_valid
        for p in range(N_PAIRS):
            sl = slice(p * LANES, (p + 1) * LANES)
            q = proj[r0:r0 + Q_BLOCK, OFF_QA + p * LANES:OFF_QA + (p + 1) * LANES]
            q2 = jnp.concatenate([jnp.where(low, q, 0.0), jnp.where(low, 0.0, q)],
                                 axis=0).astype(jnp.bfloat16)
            kb = k_s[r0:r0 + BAND, sl]
            vb = v_s[r0:r0 + BAND, sl]
            s = lax.dot_general(q2, kb, _NT, preferred_element_type=jnp.float32)
            s = jnp.where(key_ok, s + bias_ref[p], NEG)
            m = jnp.max(s, axis=-1, keepdims=True)
            e = jnp.exp2(s - m)
            den = jnp.sum(e, axis=-1, keepdims=True)
            o2 = _dot(e.astype(jnp.bfloat16), vb) / den
            o = jnp.where(low, o2[0:Q_BLOCK], o2[Q_BLOCK:2 * Q_BLOCK])
            mix_s[r0:r0 + Q_BLOCK, MIX_ATT + p * LANES:MIX_ATT + (p + 1) * LANES] = (
                o.astype(jnp.bfloat16))
            n_att_blocks += 1
            if p == 0 and c2 + 1 < ts // Q_BLOCK:
                att_projection(c2 + 1)
            else:
                late_projection(1)
            conv_cols_ready = not pieces or pieces[0] >= OFF_QR
            if conv_cols_ready and not glu_filled:
                fill_glu()
                glu_filled = True
            elif glu_filled and conv_rows_done < CONV_ROWS_IN_ATTENTION:
                conv_block(conv_rows_done)
                conv_rows_done += CONV_ROWS
            if not pieces:
                for _ in range(RET_STEPS_PER_ATT_BLOCK):
                    if ret_steps_done < ts // CHUNK:
                        retention_step(ret_steps_done)
                        ret_steps_done += 1
            att_rows_done = r0 + (Q_BLOCK if p == N_PAIRS - 1 else 0)
            ready = min(att_rows_done, conv_rows_done, ret_steps_done * CHUNK)
            while out_rows_done + OUT_ROWS <= ready:
                out_projection(out_rows_done)
                out_rows_done += OUT_ROWS
    late_projection(len(pieces))
    k_s[0:ts, :] = k_s[ts:2 * ts, :]
    v_s[0:ts, :] = v_s[ts:2 * ts, :]

    if not glu_filled:
        fill_glu()
    while conv_rows_done < ts:
        conv_block(conv_rows_done)
        conv_rows_done += CONV_ROWS
    while ret_steps_done < ts // CHUNK:
        retention_step(ret_steps_done)
        ret_steps_done += 1
    glu_s[0:CONV_HIST, :] = glu_s[ts:ts + CONV_HIST, :]
    while out_rows_done < ts:
        out_projection(out_rows_done)
        out_rows_done += OUT_ROWS


def _ffn_kernel(h_ref, p_ref, gf_ref, w1_ref, w2_ref, gp_ref, wpg_ref, wple_ref, o_ref):
    x = h_ref[...]
    hn = _rms(x, gf_ref[...]).astype(jnp.bfloat16)
    acc = jnp.zeros_like(x)
    for j in range(0, D_FF, FF_BLOCK):
        u = jnp.maximum(_dot(hn, w1_ref[:, j:j + FF_BLOCK].astype(jnp.bfloat16)), 0.0)
        acc = acc + _dot((u * u).astype(jnp.bfloat16),
                         w2_ref[j:j + FF_BLOCK, :].astype(jnp.bfloat16))
    h2 = x + acc
    gn = _rms(h2, gp_ref[...]).astype(jnp.bfloat16)
    gate = jax.nn.sigmoid(_dot(gn, wpg_ref[...].astype(jnp.bfloat16)))
    ple = _dot(p_ref[...].astype(jnp.bfloat16), wple_ref[...].astype(jnp.bfloat16))
    o_ref[...] = h2 + gate * ple


def _const_spec(shape):
    nd = len(shape)
    return pl.BlockSpec(shape, lambda *_: (0,) * nd, pipeline_mode=pl.Buffered(1))


def _layer_spec(shape, layer):
    nd = len(shape)
    return pl.BlockSpec((None,) + tuple(shape[1:]), lambda *_: (layer,) + (0,) * (nd - 1),
                        pipeline_mode=pl.Buffered(1))


def _mixer_layer(h, layer, prm, consts):
    b, s, _ = h.shape
    ts = SEQ_TILE
    nt = s // ts
    n_steps = b * nt
    h3 = h.reshape(n_steps, ts, D_MODEL)
    stacked = [prm[k] for k in ("g_mix", "w_in", "qn_g", "kn_g", "bias", "conv_w", "conv_b",
                                "ln_g", "ln_b", "pw_w", "pw_b")]
    tables = [consts[k] for k in ("dmat", "qdec", "kdec", "cdec")]
    cur_spec = pl.BlockSpec((None, ts, D_MODEL), lambda i: (i, 0, 0))
    rope_spec = pl.BlockSpec((ts, LANES), lambda i: (lax.rem(i, nt), 0))
    in_specs = ([cur_spec] + [_layer_spec(a.shape, layer) for a in stacked]
                + [rope_spec, rope_spec] + [_const_spec(a.shape) for a in tables]
                + [_layer_spec(prm["gn_g"].shape, layer), _layer_spec(prm["w_o"].shape, layer)])
    out = pl.pallas_call(
        partial(_mixer_kernel, tiles_per_seq=nt),
        grid=(n_steps,),
        in_specs=in_specs,
        out_specs=cur_spec,
        out_shape=jax.ShapeDtypeStruct(h3.shape, jnp.float32),
        scratch_shapes=[
            pltpu.VMEM((ts, D_MODEL), jnp.bfloat16),
            pltpu.VMEM((ts, D_IN), jnp.float32),
            pltpu.VMEM((2 * ts, D_ATT), jnp.bfloat16),
            pltpu.VMEM((2 * ts, D_ATT), jnp.bfloat16),
            pltpu.VMEM((CONV_HIST + ts, D_CONV), jnp.float32),
            pltpu.VMEM((N_PAIRS, LANES, LANES), jnp.float32),
            pltpu.VMEM((ts, D_MODEL), jnp.bfloat16),
        ],
        compiler_params=pltpu.CompilerParams(
            dimension_semantics=("arbitrary",),
            vmem_limit_bytes=VMEM_LIMIT_BYTES),
        name="mixer_layer",
    )(h3, *stacked, consts["cos"], consts["sin"], *tables, prm["gn_g"], prm["w_o"])
    return out.reshape(b, s, D_MODEL)


def _ffn_layer(h, layer, prm):
    b, s, _ = h.shape
    t = b * s
    stacked = [prm[k] for k in ("g_ffn", "w1", "w2", "g_ple", "w_pg", "w_ple")]
    in_specs = ([pl.BlockSpec((FFN_TILE, D_MODEL), lambda i: (i, 0)),
                 pl.BlockSpec((None, FFN_TILE, D_PLE), lambda i: (layer, i, 0))]
                + [_layer_spec(a.shape, layer) for a in stacked])
    out = pl.pallas_call(
        _ffn_kernel,
        grid=(t // FFN_TILE,),
        in_specs=in_specs,
        out_specs=pl.BlockSpec((FFN_TILE, D_MODEL), lambda i: (i, 0)),
        out_shape=jax.ShapeDtypeStruct((t, D_MODEL), jnp.float32),
        compiler_params=pltpu.CompilerParams(
            dimension_semantics=("arbitrary",),
            vmem_limit_bytes=VMEM_LIMIT_BYTES),
        name="ffn_ple_layer",
    )(h.reshape(t, D_MODEL), prm["p"], *stacked)
    return out.reshape(b, s, D_MODEL)


def _pair_lanes(per_head):
    x = np.repeat(per_head[..., None], HEAD_DIM, axis=-1)
    x = x.reshape((N_PAIRS, 2) + x.shape[1:])
    return np.concatenate([x[:, 0], x[:, 1]], axis=-1)


def _retention_consts(s):
    pos = np.arange(s, dtype=np.float64)
    inv_freq = ROPE_THETA ** (-np.arange(0, HEAD_DIM, 2, dtype=np.float64) / HEAD_DIM)
    ang = pos[:, None] * inv_freq[None, :]
    cos, sin = np.cos(ang), np.sin(ang)
    cos4 = np.concatenate([cos, cos, cos, cos], axis=-1)
    sin4 = np.concatenate([-sin, sin, -sin, sin], axis=-1)
    log_gamma = np.log(1.0 - 2.0 ** (-5.0 - np.arange(H_RET, dtype=np.float64)))
    n = np.arange(CHUNK, dtype=np.float64)
    dist = np.abs(n[:, None] - n[None, :])
    d_intra = np.exp(log_gamma[:, None, None] * dist[None])
    d_pair = d_intra.reshape(N_PAIRS, 2, CHUNK, CHUNK)
    dmat = np.concatenate([d_pair[:, 0], d_pair[:, 1]], axis=-1)
    q_dec = np.exp(log_gamma[:, None] * (n + 1.0)[None, :])
    k_dec = np.exp(log_gamma[:, None] * (CHUNK - 1 - n)[None, :])
    chunk_decay = np.exp(log_gamma * CHUNK).reshape(N_PAIRS, 2)
    qdec = _pair_lanes(q_dec)
    kdec = _pair_lanes(k_dec)
    ones = np.ones((HEAD_DIM, HEAD_DIM))
    cdec = np.stack([np.kron(np.diag(cd), ones) for cd in chunk_decay])
    tabs = dict(cos=cos4, sin=sin4, dmat=dmat, qdec=qdec, kdec=kdec, cdec=cdec)
    return {k: jnp.asarray(np.ascontiguousarray(v), dtype=jnp.float32) for k, v in tabs.items()}


def _attention_bias(rel_bias):
    lead = rel_bias.shape[:-1]
    own = (N_LEFT_CHUNKS + 1) * CHUNK
    n_m = own + CHUNK - 1
    n_far = N_LEFT_CHUNKS * CHUNK + CHUNK - REL_CLIP
    n_near = n_m - n_far
    rb = rel_bias.astype(jnp.float32) * LOG2E
    far = jnp.broadcast_to(rb[..., 2 * REL_CLIP:], lead + (n_far,))
    near = rb[..., 2 * REL_CLIP - n_near:2 * REL_CLIP][..., ::-1]
    vec = jnp.concatenate([far, near, jnp.zeros(lead + (1,), jnp.float32)], axis=-1)
    skew = jnp.tile(vec, CHUNK)[..., :CHUNK * n_m].reshape(lead + (CHUNK, n_m))
    band = skew[..., CHUNK - 1:CHUNK - 1 + own]
    neg = jnp.full(lead + (CHUNK, BAND - own), NEG, jnp.float32)
    per_head = jnp.concatenate([jnp.concatenate([band, neg], axis=-1),
                                jnp.concatenate([neg, band], axis=-1)], axis=-2)
    return per_head.reshape(lead[0], N_PAIRS, 2 * Q_BLOCK, BAND)


def kernel(x, p, norm_mix_g, w_in, qn_g, kn_g, rel_bias, conv_w, conv_b, conv_ln_g,
           conv_ln_b, conv_pw_w, conv_pw_b, ret_gn_g, w_o, norm_ffn_g, w1, w2,
           norm_ple_g, w_pg, w_ple):
    depth = w_in.shape[0]
    b, s, _ = x.shape
    assert s % SEQ_TILE == 0 and (b * s) % FFN_TILE == 0
    bf16 = jnp.bfloat16
    row = lambda a: a[:, None, :]
    prm = dict(
        g_mix=row(norm_mix_g), w_in=w_in.astype(bf16),
        qn_g=row(jnp.tile(qn_g, (1, 2))), kn_g=row(jnp.tile(kn_g, (1, 2))),
        bias=_attention_bias(rel_bias),
        conv_w=jnp.pad(conv_w, ((0, 0), (0, 1), (0, 0))), conv_b=row(conv_b),
        ln_g=row(conv_ln_g), ln_b=row(conv_ln_b),
        pw_w=conv_pw_w.astype(bf16), pw_b=row(conv_pw_b),
        gn_g=ret_gn_g.reshape(depth, N_PAIRS, 1, LANES), w_o=w_o.astype(bf16),
        g_ffn=row(norm_ffn_g), w1=w1, w2=w2,
        g_ple=row(norm_ple_g), w_pg=w_pg, w_ple=w_ple,
        p=p.reshape(depth, b * s, D_PLE),
    )
    consts = _retention_consts(s)
    h = x
    for layer in range(depth):
        h = _mixer_layer(h, layer, prm, consts)
        h = _ffn_layer(h, layer, prm)
    return h
```

```python
import math
from functools import partial

import numpy as np
import jax
import jax.numpy as jnp
from jax import lax
from jax.experimental import pallas as pl
from jax.experimental.pallas import tpu as pltpu

D_MODEL = 1024
CHUNK = 64
N_LEFT_CHUNKS = 8
HEAD_DIM = 64
D_ATT = 384
D_RET = 384
D_CONV = 256
H_ATT = D_ATT // HEAD_DIM
H_RET = D_RET // HEAD_DIM
CONV_K = 31
REL_CLIP = 128
D_FF = 4 * D_MODEL
D_PLE = 256
ROPE_THETA = 10000.0
EPS = 1e-6
D_IN = 3 * D_ATT + 2 * D_CONV + 4 * D_RET

OFF_QA, OFF_KA, OFF_VA = 0, D_ATT, 2 * D_ATT
OFF_CA = 3 * D_ATT
OFF_CG = OFF_CA + D_CONV
OFF_QR = OFF_CA + 2 * D_CONV
OFF_KR = OFF_QR + D_RET
OFF_VR = OFF_KR + D_RET
OFF_GR = OFF_VR + D_RET
MIX_ATT, MIX_CONV, MIX_RET = 0, D_ATT, D_ATT + D_CONV

LANES = 128
SUBLANES = 8
N_PAIRS = D_ATT // LANES
SEQ_TILE = N_LEFT_CHUNKS * CHUNK
Q_BLOCK = 2 * CHUNK
BAND = (N_LEFT_CHUNKS + 2) * CHUNK
CONV_HIST = 32
CONV_ROWS = 64
NORM_ROWS = 64
ATT_PROJ_COLS = 384
LATE_PROJ_COLS = 256
RET_AFTER_ATT_BLOCKS = 8
RET_STEPS_PER_ATT_BLOCK = 2
OUT_ROWS = SEQ_TILE
CONV_ROWS_IN_ATTENTION = 512
FFN_TILE = 512
FF_BLOCK = 512
NEG = -1e30
LOG2E = math.log2(math.e)
VMEM_LIMIT_BYTES = 56 * 1024 * 1024

_NT = (((1,), (1,)), ((), ()))
_TN = (((0,), (0,)), ((), ()))


def _rms(x, g):
    return (x * lax.rsqrt(jnp.mean(x * x, axis=-1, keepdims=True) + EPS)) * g


def _dot(a, b):
    return jnp.dot(a, b, preferred_element_type=jnp.float32)


def _pair_sum(x, low):
    s0 = jnp.sum(jnp.where(low, x, 0.0), axis=-1, keepdims=True)
    s1 = jnp.sum(jnp.where(low, 0.0, x), axis=-1, keepdims=True)
    return jnp.where(low, s0, s1)


def _pair_rms(x, low, g):
    ms = _pair_sum(x * x, low) * (1.0 / HEAD_DIM)
    return (x * lax.rsqrt(ms + EPS)) * g


def _swap_halves(x, first_half):
    return jnp.where(first_half, pltpu.roll(x, 96, 1), pltpu.roll(x, 32, 1))


def _mixer_kernel(hc_ref, g_ref, w_in_ref, qn_ref, kn_ref, bias_ref,
                  cw_ref, cb_ref, lng_ref, lnb_ref, pww_ref, pwb_ref,
                  cos_ref, sin_ref, dmat_ref, qdec_ref, kdec_ref, cdec_ref, gn_ref,
                  w_o_ref, o_ref, xn_s, proj, k_s, v_s, glu_s, st_s, mix_s, *, tiles_per_seq):
    ts = SEQ_TILE
    t = lax.rem(pl.program_id(0), tiles_per_seq)

    @pl.when(t == 0)
    def _():
        k_s[0:ts, :] = jnp.zeros((ts, D_ATT), jnp.bfloat16)
        v_s[0:ts, :] = jnp.zeros((ts, D_ATT), jnp.bfloat16)
        glu_s[0:CONV_HIST, :] = jnp.zeros((CONV_HIST, D_CONV), jnp.float32)
        st_s[...] = jnp.zeros_like(st_s)

    xn_s[...] = _rms(hc_ref[...], g_ref[...]).astype(jnp.bfloat16)
    for c0 in range(0, OFF_CA, ATT_PROJ_COLS):
        proj[:, c0:c0 + ATT_PROJ_COLS] = _dot(xn_s[...], w_in_ref[:, c0:c0 + ATT_PROJ_COLS])
    pieces = list(range(OFF_CA, D_IN, LATE_PROJ_COLS))

    def late_projection(n):
        for _ in range(n):
            if pieces:
                c0 = pieces.pop(0)
                proj[:, c0:c0 + LATE_PROJ_COLS] = _dot(xn_s[...],
                                                       w_in_ref[:, c0:c0 + LATE_PROJ_COLS])

    lane = lax.broadcasted_iota(jnp.int32, (1, LANES), 1)
    low = lane < HEAD_DIM
    first_half = (lane & 32) == 0

    def fill_glu():
        a = proj[:, OFF_CA:OFF_CA + D_CONV]
        gate = proj[:, OFF_CG:OFF_CG + D_CONV]
        glu_s[CONV_HIST:CONV_HIST + ts, :] = a * jax.nn.sigmoid(gate)

    base = CONV_HIST - (CONV_K - 1)

    def conv_block(r):
        acc = None
        for rho in range(SUBLANES):
            rows = CONV_ROWS + (SUBLANES if rho else 0)
            z = None
            for kk in range(CONV_K):
                if (base + kk) % SUBLANES != rho:
                    continue
                off = r + base + kk - rho
                term = cw_ref[kk:kk + 1, :] * glu_s[off:off + rows, :]
                z = term if z is None else z + term
            z = z[rho:rho + CONV_ROWS]
            acc = z if acc is None else acc + z
        y = acc + cb_ref[...]
        mu = jnp.mean(y, axis=-1, keepdims=True)
        yc = y - mu
        var = jnp.mean(yc * yc, axis=-1, keepdims=True)
        y = (yc * lax.rsqrt(var + EPS)) * lng_ref[...] + lnb_ref[...]
        y = y * jax.nn.sigmoid(y)
        y = _dot(y.astype(jnp.bfloat16), pww_ref[...]) + pwb_ref[...]
        mix_s[r:r + CONV_ROWS, MIX_CONV:MIX_CONV + D_CONV] = y.astype(jnp.bfloat16)

    row2 = lax.broadcasted_iota(jnp.int32, (2 * CHUNK, LANES), 0)
    lane2 = lax.broadcasted_iota(jnp.int32, (2 * CHUNK, LANES), 1)
    own2 = (row2 < CHUNK) == (lane2 < HEAD_DIM)

    def retention_step(n):
        rs = slice(n * CHUNK, (n + 1) * CHUNK)
        cos = cos_ref[rs, :]
        sin = sin_ref[rs, :]
        for p in range(N_PAIRS):
            q = proj[rs, OFF_QR + p * LANES:OFF_QR + (p + 1) * LANES]
            k = proj[rs, OFF_KR + p * LANES:OFF_KR + (p + 1) * LANES]
            v = proj[rs, OFF_VR + p * LANES:OFF_VR + (p + 1) * LANES]
            g = proj[rs, OFF_GR + p * LANES:OFF_GR + (p + 1) * LANES]
            qr = (q * cos + _swap_halves(q, first_half) * sin) * (HEAD_DIM ** -0.5)
            kr = k * cos + _swap_halves(k, first_half) * sin
            k2 = jnp.where(own2, jnp.concatenate([kr, kr], axis=0), 0.0).astype(jnp.bfloat16)
            v2 = jnp.where(own2, jnp.concatenate([v, v], axis=0), 0.0).astype(jnp.bfloat16)
            s = lax.dot_general(qr.astype(jnp.bfloat16), k2, _NT,
                                preferred_element_type=jnp.float32)
            sd = (s * dmat_ref[p]).astype(jnp.bfloat16)
            qd = (qr * qdec_ref[p]).astype(jnp.bfloat16)
            state = st_s[p]
            lhs = jnp.concatenate([qd, sd], axis=1)
            rhs = jnp.concatenate([state.astype(jnp.bfloat16), v2], axis=0)
            o = _dot(lhs, rhs)
            cen = o - _pair_sum(o, low) * (1.0 / HEAD_DIM)
            var = _pair_sum(cen * cen, low) * (1.0 / HEAD_DIM)
            y = (cen * lax.rsqrt(var + EPS)) * gn_ref[p]
            y = (g * jax.nn.sigmoid(g)) * y
            mix_s[rs, MIX_RET + p * LANES:MIX_RET + (p + 1) * LANES] = y.astype(jnp.bfloat16)
            kd = (kr * kdec_ref[p]).astype(jnp.bfloat16)
            kv = lax.dot_general(kd, v.astype(jnp.bfloat16), _TN,
                                 preferred_element_type=jnp.float32)
            st_s[p] = cdec_ref[p] * state + jnp.where(own2, kv, 0.0)

    for p in range(N_PAIRS):
        for r0 in range(0, ts, NORM_ROWS):
            rs = slice(r0, r0 + NORM_ROWS)
            qc = slice(OFF_QA + p * LANES, OFF_QA + (p + 1) * LANES)
            kc = slice(OFF_KA + p * LANES, OFF_KA + (p + 1) * LANES)
            proj[rs, qc] = _pair_rms(proj[rs, qc], low, qn_ref[...]) * (HEAD_DIM ** -0.5 * LOG2E)
            k_s[ts + r0:ts + r0 + NORM_ROWS, p * LANES:(p + 1) * LANES] = (
                _pair_rms(proj[rs, kc], low, kn_ref[...]).astype(jnp.bfloat16))
    for r0 in range(0, ts, NORM_ROWS):
        v_s[ts + r0:ts + r0 + NORM_ROWS, :] = (
            proj[r0:r0 + NORM_ROWS, OFF_VA:OFF_VA + D_ATT].astype(jnp.bfloat16))

    col = lax.broadcasted_iota(jnp.int32, (1, BAND), 1)
    n_att_blocks = 0
    glu_filled = False
    conv_rows_done = 0
    ret_steps_done = 0
    out_rows_done = 0

    def out_projection(r):
        o_ref[r:r + OUT_ROWS, :] = hc_ref[r:r + OUT_ROWS, :] + _dot(mix_s[r:r + OUT_ROWS, :],
                                                                     w_o_ref[...])
    for c2 in range(ts // Q_BLOCK):
        r0 = c2 * Q_BLOCK
        first_valid = jnp.where(t == 0, ts - r0, 0)
        key_ok = col >= first_valid
        for p in range(N_PAIRS):
            sl = slice(p * LANES, (p + 1) * LANES)
            q = proj[r0:r0 + Q_BLOCK, OFF_QA + p * LANES:OFF_QA + (p + 1) * LANES]
            q2 = jnp.concatenate([jnp.where(low, q, 0.0), jnp.where(low, 0.0, q)],
                                 axis=0).astype(jnp.bfloat16)
            kb = k_s[r0:r0 + BAND, sl]
            vb = v_s[r0:r0 + BAND, sl]
            s = lax.dot_general(q2, kb, _NT, preferred_element_type=jnp.float32)
            s = jnp.where(key_ok, s + bias_ref[p], NEG)
            m = jnp.max(s, axis=-1, keepdims=True)
            e = jnp.exp2(s - m)
            den = jnp.sum(e, axis=-1, keepdims=True)
            o2 = _dot(e.astype(jnp.bfloat16), vb) / den
            o = jnp.where(low, o2[0:Q_BLOCK], o2[Q_BLOCK:2 * Q_BLOCK])
            mix_s[r0:r0 + Q_BLOCK, MIX_ATT + p * LANES:MIX_ATT + (p + 1) * LANES] = (
                o.astype(jnp.bfloat16))
            n_att_blocks += 1
            late_projection(1)
            conv_cols_ready = not pieces or pieces[0] >= OFF_QR
            if conv_cols_ready and not glu_filled:
                fill_glu()
                glu_filled = True
            elif glu_filled and conv_rows_done < CONV_ROWS_IN_ATTENTION:
                conv_block(conv_rows_done)
                conv_rows_done += CONV_ROWS
            if not pieces:
                for _ in range(RET_STEPS_PER_ATT_BLOCK):
                    if n_att_blocks > RET_AFTER_ATT_BLOCKS and ret_steps_done < ts // CHUNK:
                        retention_step(ret_steps_done)
                        ret_steps_done += 1
            att_rows_done = r0 + (Q_BLOCK if p == N_PAIRS - 1 else 0)
            ready = min(att_rows_done, conv_rows_done, ret_steps_done * CHUNK)
            while out_rows_done + OUT_ROWS <= ready:
                out_projection(out_rows_done)
                out_rows_done += OUT_ROWS
    late_projection(len(pieces))
    k_s[0:ts, :] = k_s[ts:2 * ts, :]
    v_s[0:ts, :] = v_s[ts:2 * ts, :]

    if not glu_filled:
        fill_glu()
    while conv_rows_done < ts:
        conv_block(conv_rows_done)
        conv_rows_done += CONV_ROWS
    while ret_steps_done < ts // CHUNK:
        retention_step(ret_steps_done)
        ret_steps_done += 1
    glu_s[0:CONV_HIST, :] = glu_s[ts:ts + CONV_HIST, :]
    while out_rows_done < ts:
        out_projection(out_rows_done)
        out_rows_done += OUT_ROWS


def _ffn_kernel(h_ref, p_ref, gf_ref, w1_ref, w2_ref, gp_ref, wpg_ref, wple_ref, o_ref):
    x = h_ref[...]
    hn = _rms(x, gf_ref[...]).astype(jnp.bfloat16)
    acc = jnp.zeros_like(x)
    for j in range(0, D_FF, FF_BLOCK):
        u = jnp.maximum(_dot(hn, w1_ref[:, j:j + FF_BLOCK].astype(jnp.bfloat16)), 0.0)
        acc = acc + _dot((u * u).astype(jnp.bfloat16),
                         w2_ref[j:j + FF_BLOCK, :].astype(jnp.bfloat16))
    h2 = x + acc
    gn = _rms(h2, gp_ref[...]).astype(jnp.bfloat16)
    gate = jax.nn.sigmoid(_dot(gn, wpg_ref[...].astype(jnp.bfloat16)))
    ple = _dot(p_ref[...].astype(jnp.bfloat16), wple_ref[...].astype(jnp.bfloat16))
    o_ref[...] = h2 + gate * ple


def _const_spec(shape):
    nd = len(shape)
    return pl.BlockSpec(shape, lambda *_: (0,) * nd, pipeline_mode=pl.Buffered(1))


def _layer_spec(shape, layer):
    nd = len(shape)
    return pl.BlockSpec((None,) + tuple(shape[1:]), lambda *_: (layer,) + (0,) * (nd - 1),
                        pipeline_mode=pl.Buffered(1))


def _mixer_layer(h, layer, prm, consts):
    b, s, _ = h.shape
    ts = SEQ_TILE
    nt = s // ts
    n_steps = b * nt
    h3 = h.reshape(n_steps, ts, D_MODEL)
    stacked = [prm[k] for k in ("g_mix", "w_in", "qn_g", "kn_g", "bias", "conv_w", "conv_b",
                                "ln_g", "ln_b", "pw_w", "pw_b")]
    tables = [consts[k] for k in ("dmat", "qdec", "kdec", "cdec")]
    cur_spec = pl.BlockSpec((None, ts, D_MODEL), lambda i: (i, 0, 0))
    rope_spec = pl.BlockSpec((ts, LANES), lambda i: (lax.rem(i, nt), 0))
    in_specs = ([cur_spec] + [_layer_spec(a.shape, layer) for a in stacked]
                + [rope_spec, rope_spec] + [_const_spec(a.shape) for a in tables]
                + [_layer_spec(prm["gn_g"].shape, layer), _layer_spec(prm["w_o"].shape, layer)])
    out = pl.pallas_call(
        partial(_mixer_kernel, tiles_per_seq=nt),
        grid=(n_steps,),
        in_specs=in_specs,
        out_specs=cur_spec,
        out_shape=jax.ShapeDtypeStruct(h3.shape, jnp.float32),
        scratch_shapes=[
            pltpu.VMEM((ts, D_MODEL), jnp.bfloat16),
            pltpu.VMEM((ts, D_IN), jnp.float32),
            pltpu.VMEM((2 * ts, D_ATT), jnp.bfloat16),
            pltpu.VMEM((2 * ts, D_ATT), jnp.bfloat16),
            pltpu.VMEM((CONV_HIST + ts, D_CONV), jnp.float32),
            pltpu.VMEM((N_PAIRS, LANES, LANES), jnp.float32),
            pltpu.VMEM((ts, D_MODEL), jnp.bfloat16),
        ],
        compiler_params=pltpu.CompilerParams(
            dimension_semantics=("arbitrary",),
            vmem_limit_bytes=VMEM_LIMIT_BYTES),
        name="mixer_layer",
    )(h3, *stacked, consts["cos"], consts["sin"], *tables, prm["gn_g"], prm["w_o"])
    return out.reshape(b, s, D_MODEL)


def _ffn_layer(h, layer, prm):
    b, s, _ = h.shape
    t = b * s
    stacked = [prm[k] for k in ("g_ffn", "w1", "w2", "g_ple", "w_pg", "w_ple")]
    in_specs = ([pl.BlockSpec((FFN_TILE, D_MODEL), lambda i: (i, 0)),
                 pl.BlockSpec((None, FFN_TILE, D_PLE), lambda i: (layer, i, 0))]
                + [_layer_spec(a.shape, layer) for a in stacked])
    out = pl.pallas_call(
        _ffn_kernel,
        grid=(t // FFN_TILE,),
        in_specs=in_specs,
        out_specs=pl.BlockSpec((FFN_TILE, D_MODEL), lambda i: (i, 0)),
        out_shape=jax.ShapeDtypeStruct((t, D_MODEL), jnp.float32),
        compiler_params=pltpu.CompilerParams(
            dimension_semantics=("arbitrary",),
            vmem_limit_bytes=VMEM_LIMIT_BYTES),
        name="ffn_ple_layer",
    )(h.reshape(t, D_MODEL), prm["p"], *stacked)
    return out.reshape(b, s, D_MODEL)


def _pair_lanes(per_head):
    x = np.repeat(per_head[..., None], HEAD_DIM, axis=-1)
    x = x.reshape((N_PAIRS, 2) + x.shape[1:])
    return np.concatenate([x[:, 0], x[:, 1]], axis=-1)


def _retention_consts(s):
    pos = np.arange(s, dtype=np.float64)
    inv_freq = ROPE_THETA ** (-np.arange(0, HEAD_DIM, 2, dtype=np.float64) / HEAD_DIM)
    ang = pos[:, None] * inv_freq[None, :]
    cos, sin = np.cos(ang), np.sin(ang)
    cos4 = np.concatenate([cos, cos, cos, cos], axis=-1)
    sin4 = np.concatenate([-sin, sin, -sin, sin], axis=-1)
    log_gamma = np.log(1.0 - 2.0 ** (-5.0 - np.arange(H_RET, dtype=np.float64)))
    n = np.arange(CHUNK, dtype=np.float64)
    dist = np.abs(n[:, None] - n[None, :])
    d_intra = np.exp(log_gamma[:, None, None] * dist[None])
    d_pair = d_intra.reshape(N_PAIRS, 2, CHUNK, CHUNK)
    dmat = np.concatenate([d_pair[:, 0], d_pair[:, 1]], axis=-1)
    q_dec = np.exp(log_gamma[:, None] * (n + 1.0)[None, :])
    k_dec = np.exp(log_gamma[:, None] * (CHUNK - 1 - n)[None, :])
    chunk_decay = np.exp(log_gamma * CHUNK).reshape(N_PAIRS, 2)
    qdec = _pair_lanes(q_dec)
    kdec = _pair_lanes(k_dec)
    ones = np.ones((HEAD_DIM, HEAD_DIM))
    cdec = np.stack([np.kron(np.diag(cd), ones) for cd in chunk_decay])
    tabs = dict(cos=cos4, sin=sin4, dmat=dmat, qdec=qdec, kdec=kdec, cdec=cdec)
    return {k: jnp.asarray(np.ascontiguousarray(v), dtype=jnp.float32) for k, v in tabs.items()}


def _attention_bias(rel_bias):
    lead = rel_bias.shape[:-1]
    own = (N_LEFT_CHUNKS + 1) * CHUNK
    n_m = own + CHUNK - 1
    n_far = N_LEFT_CHUNKS * CHUNK + CHUNK - REL_CLIP
    n_near = n_m - n_far
    rb = rel_bias.astype(jnp.float32) * LOG2E
    far = jnp.broadcast_to(rb[..., 2 * REL_CLIP:], lead + (n_far,))
    near = rb[..., 2 * REL_CLIP - n_near:2 * REL_CLIP][..., ::-1]
    vec = jnp.concatenate([far, near, jnp.zeros(lead + (1,), jnp.float32)], axis=-1)
    skew = jnp.tile(vec, CHUNK)[..., :CHUNK * n_m].reshape(lead + (CHUNK, n_m))
    band = skew[..., CHUNK - 1:CHUNK - 1 + own]
    neg = jnp.full(lead + (CHUNK, BAND - own), NEG, jnp.float32)
    per_head = jnp.concatenate([jnp.concatenate([band, neg], axis=-1),
                                jnp.concatenate([neg, band], axis=-1)], axis=-2)
    return per_head.reshape(lead[0], N_PAIRS, 2 * Q_BLOCK, BAND)


def kernel(x, p, norm_mix_g, w_in, qn_g, kn_g, rel_bias, conv_w, conv_b, conv_ln_g,
           conv_ln_b, conv_pw_w, conv_pw_b, ret_gn_g, w_o, norm_ffn_g, w1, w2,
           norm_ple_g, w_pg, w_ple):
    depth = w_in.shape[0]
    b, s, _ = x.shape
    assert s % SEQ_TILE == 0 and (b * s) % FFN_TILE == 0
    bf16 = jnp.bfloat16
    row = lambda a: a[:, None, :]
    prm = dict(
        g_mix=row(norm_mix_g), w_in=w_in.astype(bf16),
        qn_g=row(jnp.tile(qn_g, (1, 2))), kn_g=row(jnp.tile(kn_g, (1, 2))),
        bias=_attention_bias(rel_bias),
        conv_w=jnp.pad(conv_w, ((0, 0), (0, 1), (0, 0))), conv_b=row(conv_b),
        ln_g=row(conv_ln_g), ln_b=row(conv_ln_b),
        pw_w=conv_pw_w.astype(bf16), pw_b=row(conv_pw_b),
        gn_g=ret_gn_g.reshape(depth, N_PAIRS, 1, LANES), w_o=w_o.astype(bf16),
        g_ffn=row(norm_ffn_g), w1=w1, w2=w2,
        g_ple=row(norm_ple_g), w_pg=w_pg, w_ple=w_ple,
        p=p.reshape(depth, b * s, D_PLE),
    )
    consts = _retention_consts(s)
    h = x
    for layer in range(depth):
        h = _mixer_layer(h, layer, prm, consts)
        h = _ffn_layer(h, layer, prm)
    return h
```

```python
import math
from functools import partial

import numpy as np
import jax
import jax.numpy as jnp
from jax import lax
from jax.experimental import pallas as pl
from jax.experimental.pallas import tpu as pltpu

D_MODEL = 1024
CHUNK = 64
N_LEFT_CHUNKS = 8
HEAD_DIM = 64
D_ATT = 384
D_RET = 384
D_CONV = 256
H_ATT = D_ATT // HEAD_DIM
H_RET = D_RET // HEAD_DIM
CONV_K = 31
REL_CLIP = 128
D_FF = 4 * D_MODEL
D_PLE = 256
ROPE_THETA = 10000.0
EPS = 1e-6
D_IN = 3 * D_ATT + 2 * D_CONV + 4 * D_RET

OFF_QA, OFF_KA, OFF_VA = 0, D_ATT, 2 * D_ATT
OFF_CA = 3 * D_ATT
OFF_CG = OFF_CA + D_CONV
OFF_QR = OFF_CA + 2 * D_CONV
OFF_KR = OFF_QR + D_RET
OFF_VR = OFF_KR + D_RET
OFF_GR = OFF_VR + D_RET
MIX_ATT, MIX_CONV, MIX_RET = 0, D_ATT, D_ATT + D_CONV

LANES = 128
SUBLANES = 8
N_PAIRS = D_ATT // LANES
SEQ_TILE = N_LEFT_CHUNKS * CHUNK
Q_BLOCK = 2 * CHUNK
BAND = (N_LEFT_CHUNKS + 2) * CHUNK
CONV_HIST = 32
CONV_ROWS = 64
NORM_ROWS = 64
ATT_PROJ_COLS = 384
LATE_PROJ_COLS = 256
RET_AFTER_ATT_BLOCKS = 8
RET_STEPS_PER_ATT_BLOCK = 2
OUT_ROWS = SEQ_TILE
CONV_ROWS_IN_ATTENTION = 512
FFN_TILE = 512
FF_BLOCK = 512
NEG = -1e30
LOG2E = math.log2(math.e)
VMEM_LIMIT_BYTES = 56 * 1024 * 1024

_NT = (((1,), (1,)), ((), ()))
_TN = (((0,), (0,)), ((), ()))


def _rms(x, g):
    return (x * lax.rsqrt(jnp.mean(x * x, axis=-1, keepdims=True) + EPS)) * g


def _dot(a, b):
    return jnp.dot(a, b, preferred_element_type=jnp.float32)


def _pair_sum(x, low):
    s0 = jnp.sum(jnp.where(low, x, 0.0), axis=-1, keepdims=True)
    s1 = jnp.sum(jnp.where(low, 0.0, x), axis=-1, keepdims=True)
    return jnp.where(low, s0, s1)


def _pair_rms(x, low, g):
    ms = _pair_sum(x * x, low) * (1.0 / HEAD_DIM)
    return (x * lax.rsqrt(ms + EPS)) * g


def _swap_halves(x, first_half):
    return jnp.where(first_half, pltpu.roll(x, 96, 1), pltpu.roll(x, 32, 1))


def _mixer_kernel(hc_ref, g_ref, w_in_ref, qn_ref, kn_ref, bias_ref,
                  cw_ref, cb_ref, lng_ref, lnb_ref, pww_ref, pwb_ref,
                  cos_ref, sin_ref, dmat_ref, qdec_ref, kdec_ref, cdec_ref, gn_ref,
                  w_o_ref, o_ref, xn_s, proj, k_s, v_s, glu_s, st_s, mix_s, *, tiles_per_seq):
    ts = SEQ_TILE
    t = lax.rem(pl.program_id(0), tiles_per_seq)

    @pl.when(t == 0)
    def _():
        k_s[0:ts, :] = jnp.zeros((ts, D_ATT), jnp.bfloat16)
        v_s[0:ts, :] = jnp.zeros((ts, D_ATT), jnp.bfloat16)
        glu_s[0:CONV_HIST, :] = jnp.zeros((CONV_HIST, D_CONV), jnp.float32)
        st_s[...] = jnp.zeros_like(st_s)

    xn_s[...] = _rms(hc_ref[...], g_ref[...]).astype(jnp.bfloat16)
    for c0 in range(0, OFF_CA, ATT_PROJ_COLS):
        proj[:, c0:c0 + ATT_PROJ_COLS] = _dot(
            xn_s[...], w_in_ref[:, c0:c0 + ATT_PROJ_COLS].astype(jnp.bfloat16))
    pieces = list(range(OFF_CA, D_IN, LATE_PROJ_COLS))

    def late_projection(n):
        for _ in range(n):
            if pieces:
                c0 = pieces.pop(0)
                proj[:, c0:c0 + LATE_PROJ_COLS] = _dot(
                    xn_s[...], w_in_ref[:, c0:c0 + LATE_PROJ_COLS].astype(jnp.bfloat16))

    lane = lax.broadcasted_iota(jnp.int32, (1, LANES), 1)
    low = lane < HEAD_DIM
    first_half = (lane & 32) == 0

    def fill_glu():
        a = proj[:, OFF_CA:OFF_CA + D_CONV]
        gate = proj[:, OFF_CG:OFF_CG + D_CONV]
        glu_s[CONV_HIST:CONV_HIST + ts, :] = a * jax.nn.sigmoid(gate)

    base = CONV_HIST - (CONV_K - 1)

    def conv_block(r):
        acc = None
        for rho in range(SUBLANES):
            rows = CONV_ROWS + (SUBLANES if rho else 0)
            z = None
            for kk in range(CONV_K):
                if (base + kk) % SUBLANES != rho:
                    continue
                off = r + base + kk - rho
                term = cw_ref[kk:kk + 1, :] * glu_s[off:off + rows, :]
                z = term if z is None else z + term
            z = z[rho:rho + CONV_ROWS]
            acc = z if acc is None else acc + z
        y = acc + cb_ref[...]
        mu = jnp.mean(y, axis=-1, keepdims=True)
        yc = y - mu
        var = jnp.mean(yc * yc, axis=-1, keepdims=True)
        y = (yc * lax.rsqrt(var + EPS)) * lng_ref[...] + lnb_ref[...]
        y = y * jax.nn.sigmoid(y)
        y = _dot(y.astype(jnp.bfloat16), pww_ref[...]) + pwb_ref[...]
        mix_s[r:r + CONV_ROWS, MIX_CONV:MIX_CONV + D_CONV] = y.astype(jnp.bfloat16)

    row2 = lax.broadcasted_iota(jnp.int32, (2 * CHUNK, LANES), 0)
    lane2 = lax.broadcasted_iota(jnp.int32, (2 * CHUNK, LANES), 1)
    own2 = (row2 < CHUNK) == (lane2 < HEAD_DIM)

    def retention_step(n):
        rs = slice(n * CHUNK, (n + 1) * CHUNK)
        cos = cos_ref[rs, :]
        sin = sin_ref[rs, :]
        for p in range(N_PAIRS):
            q = proj[rs, OFF_QR + p * LANES:OFF_QR + (p + 1) * LANES]
            k = proj[rs, OFF_KR + p * LANES:OFF_KR + (p + 1) * LANES]
            v = proj[rs, OFF_VR + p * LANES:OFF_VR + (p + 1) * LANES]
            g = proj[rs, OFF_GR + p * LANES:OFF_GR + (p + 1) * LANES]
            qr = (q * cos + _swap_halves(q, first_half) * sin) * (HEAD_DIM ** -0.5)
            kr = k * cos + _swap_halves(k, first_half) * sin
            k2 = jnp.where(own2, jnp.concatenate([kr, kr], axis=0), 0.0).astype(jnp.bfloat16)
            v2 = jnp.where(own2, jnp.concatenate([v, v], axis=0), 0.0).astype(jnp.bfloat16)
            s = lax.dot_general(qr.astype(jnp.bfloat16), k2, _NT,
                                preferred_element_type=jnp.float32)
            sd = (s * dmat_ref[p]).astype(jnp.bfloat16)
            qd = (qr * qdec_ref[p]).astype(jnp.bfloat16)
            state = st_s[p]
            lhs = jnp.concatenate([qd, sd], axis=1)
            rhs = jnp.concatenate([state.astype(jnp.bfloat16), v2], axis=0)
            o = _dot(lhs, rhs)
            cen = o - _pair_sum(o, low) * (1.0 / HEAD_DIM)
            var = _pair_sum(cen * cen, low) * (1.0 / HEAD_DIM)
            y = (cen * lax.rsqrt(var + EPS)) * gn_ref[p]
            y = (g * jax.nn.sigmoid(g)) * y
            mix_s[rs, MIX_RET + p * LANES:MIX_RET + (p + 1) * LANES] = y.astype(jnp.bfloat16)
            kd = (kr * kdec_ref[p]).astype(jnp.bfloat16)
            kv = lax.dot_general(kd, v.astype(jnp.bfloat16), _TN,
                                 preferred_element_type=jnp.float32)
            st_s[p] = cdec_ref[p] * state + jnp.where(own2, kv, 0.0)

    for p in range(N_PAIRS):
        for r0 in range(0, ts, NORM_ROWS):
            rs = slice(r0, r0 + NORM_ROWS)
            qc = slice(OFF_QA + p * LANES, OFF_QA + (p + 1) * LANES)
            kc = slice(OFF_KA + p * LANES, OFF_KA + (p + 1) * LANES)
            proj[rs, qc] = _pair_rms(proj[rs, qc], low, qn_ref[...]) * (HEAD_DIM ** -0.5 * LOG2E)
            k_s[ts + r0:ts + r0 + NORM_ROWS, p * LANES:(p + 1) * LANES] = (
                _pair_rms(proj[rs, kc], low, kn_ref[...]).astype(jnp.bfloat16))
    for r0 in range(0, ts, NORM_ROWS):
        v_s[ts + r0:ts + r0 + NORM_ROWS, :] = (
            proj[r0:r0 + NORM_ROWS, OFF_VA:OFF_VA + D_ATT].astype(jnp.bfloat16))

    col = lax.broadcasted_iota(jnp.int32, (1, BAND), 1)
    n_att_blocks = 0
    glu_filled = False
    conv_rows_done = 0
    ret_steps_done = 0
    out_rows_done = 0

    def out_projection(r):
        o_ref[r:r + OUT_ROWS, :] = hc_ref[r:r + OUT_ROWS, :] + _dot(
            mix_s[r:r + OUT_ROWS, :], w_o_ref[...].astype(jnp.bfloat16))
    for c2 in range(ts // Q_BLOCK):
        r0 = c2 * Q_BLOCK
        first_valid = jnp.where(t == 0, ts - r0, 0)
        key_ok = col >= first_valid
        for p in range(N_PAIRS):
            sl = slice(p * LANES, (p + 1) * LANES)
            q = proj[r0:r0 + Q_BLOCK, OFF_QA + p * LANES:OFF_QA + (p + 1) * LANES]
            q2 = jnp.concatenate([jnp.where(low, q, 0.0), jnp.where(low, 0.0, q)],
                                 axis=0).astype(jnp.bfloat16)
            kb = k_s[r0:r0 + BAND, sl]
            vb = v_s[r0:r0 + BAND, sl]
            s = lax.dot_general(q2, kb, _NT, preferred_element_type=jnp.float32)
            s = jnp.where(key_ok, s + bias_ref[p], NEG)
            m = jnp.max(s, axis=-1, keepdims=True)
            e = jnp.exp2(s - m)
            den = jnp.sum(e, axis=-1, keepdims=True)
            o2 = _dot(e.astype(jnp.bfloat16), vb) / den
            o = jnp.where(low, o2[0:Q_BLOCK], o2[Q_BLOCK:2 * Q_BLOCK])
            mix_s[r0:r0 + Q_BLOCK, MIX_ATT + p * LANES:MIX_ATT + (p + 1) * LANES] = (
                o.astype(jnp.bfloat16))
            n_att_blocks += 1
            late_projection(1)
            conv_cols_ready = not pieces or pieces[0] >= OFF_QR
            if conv_cols_ready and not glu_filled:
                fill_glu()
                glu_filled = True
            elif glu_filled and conv_rows_done < CONV_ROWS_IN_ATTENTION:
                conv_block(conv_rows_done)
                conv_rows_done += CONV_ROWS
            if not pieces:
                for _ in range(RET_STEPS_PER_ATT_BLOCK):
                    if n_att_blocks > RET_AFTER_ATT_BLOCKS and ret_steps_done < ts // CHUNK:
                        retention_step(ret_steps_done)
                        ret_steps_done += 1
            att_rows_done = r0 + (Q_BLOCK if p == N_PAIRS - 1 else 0)
            ready = min(att_rows_done, conv_rows_done, ret_steps_done * CHUNK)
            while out_rows_done + OUT_ROWS <= ready:
                out_projection(out_rows_done)
                out_rows_done += OUT_ROWS
    late_projection(len(pieces))
    k_s[0:ts, :] = k_s[ts:2 * ts, :]
    v_s[0:ts, :] = v_s[ts:2 * ts, :]

    if not glu_filled:
        fill_glu()
    while conv_rows_done < ts:
        conv_block(conv_rows_done)
        conv_rows_done += CONV_ROWS
    while ret_steps_done < ts // CHUNK:
        retention_step(ret_steps_done)
        ret_steps_done += 1
    glu_s[0:CONV_HIST, :] = glu_s[ts:ts + CONV_HIST, :]
    while out_rows_done < ts:
        out_projection(out_rows_done)
        out_rows_done += OUT_ROWS


def _ffn_kernel(h_ref, p_ref, gf_ref, w1_ref, w2_ref, gp_ref, wpg_ref, wple_ref, o_ref):
    x = h_ref[...]
    hn = _rms(x, gf_ref[...]).astype(jnp.bfloat16)
    acc = jnp.zeros_like(x)
    for j in range(0, D_FF, FF_BLOCK):
        u = jnp.maximum(_dot(hn, w1_ref[:, j:j + FF_BLOCK].astype(jnp.bfloat16)), 0.0)
        acc = acc + _dot((u * u).astype(jnp.bfloat16),
                         w2_ref[j:j + FF_BLOCK, :].astype(jnp.bfloat16))
    h2 = x + acc
    gn = _rms(h2, gp_ref[...]).astype(jnp.bfloat16)
    gate = jax.nn.sigmoid(_dot(gn, wpg_ref[...].astype(jnp.bfloat16)))
    ple = _dot(p_ref[...].astype(jnp.bfloat16), wple_ref[...].astype(jnp.bfloat16))
    o_ref[...] = h2 + gate * ple


def _const_spec(shape):
    nd = len(shape)
    return pl.BlockSpec(shape, lambda *_: (0,) * nd, pipeline_mode=pl.Buffered(1))


def _layer_spec(shape, layer):
    nd = len(shape)
    return pl.BlockSpec((None,) + tuple(shape[1:]), lambda *_: (layer,) + (0,) * (nd - 1),
                        pipeline_mode=pl.Buffered(1))


def _mixer_layer(h, layer, prm, consts):
    b, s, _ = h.shape
    ts = SEQ_TILE
    nt = s // ts
    n_steps = b * nt
    h3 = h.reshape(n_steps, ts, D_MODEL)
    stacked = [prm[k] for k in ("g_mix", "w_in", "qn_g", "kn_g", "bias", "conv_w", "conv_b",
                                "ln_g", "ln_b", "pw_w", "pw_b")]
    tables = [consts[k] for k in ("dmat", "qdec", "kdec", "cdec")]
    cur_spec = pl.BlockSpec((None, ts, D_MODEL), lambda i: (i, 0, 0))
    rope_spec = pl.BlockSpec((ts, LANES), lambda i: (lax.rem(i, nt), 0))
    in_specs = ([cur_spec] + [_layer_spec(a.shape, layer) for a in stacked]
                + [rope_spec, rope_spec] + [_const_spec(a.shape) for a in tables]
                + [_layer_spec(prm["gn_g"].shape, layer), _layer_spec(prm["w_o"].shape, layer)])
    out = pl.pallas_call(
        partial(_mixer_kernel, tiles_per_seq=nt),
        grid=(n_steps,),
        in_specs=in_specs,
        out_specs=cur_spec,
        out_shape=jax.ShapeDtypeStruct(h3.shape, jnp.float32),
        scratch_shapes=[
            pltpu.VMEM((ts, D_MODEL), jnp.bfloat16),
            pltpu.VMEM((ts, D_IN), jnp.float32),
            pltpu.VMEM((2 * ts, D_ATT), jnp.bfloat16),
            pltpu.VMEM((2 * ts, D_ATT), jnp.bfloat16),
            pltpu.VMEM((CONV_HIST + ts, D_CONV), jnp.float32),
            pltpu.VMEM((N_PAIRS, LANES, LANES), jnp.float32),
            pltpu.VMEM((ts, D_MODEL), jnp.bfloat16),
        ],
        compiler_params=pltpu.CompilerParams(
            dimension_semantics=("arbitrary",),
            vmem_limit_bytes=VMEM_LIMIT_BYTES),
        name="mixer_layer",
    )(h3, *stacked, consts["cos"], consts["sin"], *tables, prm["gn_g"], prm["w_o"])
    return out.reshape(b, s, D_MODEL)


def _ffn_layer(h, layer, prm):
    b, s, _ = h.shape
    t = b * s
    stacked = [prm[k] for k in ("g_ffn", "w1", "w2", "g_ple", "w_pg", "w_ple")]
    in_specs = ([pl.BlockSpec((FFN_TILE, D_MODEL), lambda i: (i, 0)),
                 pl.BlockSpec((None, FFN_TILE, D_PLE), lambda i: (layer, i, 0))]
                + [_layer_spec(a.shape, layer) for a in stacked])
    out = pl.pallas_call(
        _ffn_kernel,
        grid=(t // FFN_TILE,),
        in_specs=in_specs,
        out_specs=pl.BlockSpec((FFN_TILE, D_MODEL), lambda i: (i, 0)),
        out_shape=jax.ShapeDtypeStruct((t, D_MODEL), jnp.float32),
        compiler_params=pltpu.CompilerParams(
            dimension_semantics=("arbitrary",),
            vmem_limit_bytes=VMEM_LIMIT_BYTES),
        name="ffn_ple_layer",
    )(h.reshape(t, D_MODEL), prm["p"], *stacked)
    return out.reshape(b, s, D_MODEL)


def _pair_lanes(per_head):
    x = np.repeat(per_head[..., None], HEAD_DIM, axis=-1)
    x = x.reshape((N_PAIRS, 2) + x.shape[1:])
    return np.concatenate([x[:, 0], x[:, 1]], axis=-1)


def _retention_consts(s):
    pos = np.arange(s, dtype=np.float64)
    inv_freq = ROPE_THETA ** (-np.arange(0, HEAD_DIM, 2, dtype=np.float64) / HEAD_DIM)
    ang = pos[:, None] * inv_freq[None, :]
    cos, sin = np.cos(ang), np.sin(ang)
    cos4 = np.concatenate([cos, cos, cos, cos], axis=-1)
    sin4 = np.concatenate([-sin, sin, -sin, sin], axis=-1)
    log_gamma = np.log(1.0 - 2.0 ** (-5.0 - np.arange(H_RET, dtype=np.float64)))
    n = np.arange(CHUNK, dtype=np.float64)
    dist = np.abs(n[:, None] - n[None, :])
    d_intra = np.exp(log_gamma[:, None, None] * dist[None])
    d_pair = d_intra.reshape(N_PAIRS, 2, CHUNK, CHUNK)
    dmat = np.concatenate([d_pair[:, 0], d_pair[:, 1]], axis=-1)
    q_dec = np.exp(log_gamma[:, None] * (n + 1.0)[None, :])
    k_dec = np.exp(log_gamma[:, None] * (CHUNK - 1 - n)[None, :])
    chunk_decay = np.exp(log_gamma * CHUNK).reshape(N_PAIRS, 2)
    qdec = _pair_lanes(q_dec)
    kdec = _pair_lanes(k_dec)
    ones = np.ones((HEAD_DIM, HEAD_DIM))
    cdec = np.stack([np.kron(np.diag(cd), ones) for cd in chunk_decay])
    tabs = dict(cos=cos4, sin=sin4, dmat=dmat, qdec=qdec, kdec=kdec, cdec=cdec)
    return {k: jnp.asarray(np.ascontiguousarray(v), dtype=jnp.float32) for k, v in tabs.items()}


def _attention_bias(rel_bias):
    lead = rel_bias.shape[:-1]
    own = (N_LEFT_CHUNKS + 1) * CHUNK
    n_m = own + CHUNK - 1
    n_far = N_LEFT_CHUNKS * CHUNK + CHUNK - REL_CLIP
    n_near = n_m - n_far
    rb = rel_bias.astype(jnp.float32) * LOG2E
    far = jnp.broadcast_to(rb[..., 2 * REL_CLIP:], lead + (n_far,))
    near = rb[..., 2 * REL_CLIP - n_near:2 * REL_CLIP][..., ::-1]
    vec = jnp.concatenate([far, near, jnp.zeros(lead + (1,), jnp.float32)], axis=-1)
    skew = jnp.tile(vec, CHUNK)[..., :CHUNK * n_m].reshape(lead + (CHUNK, n_m))
    band = skew[..., CHUNK - 1:CHUNK - 1 + own]
    neg = jnp.full(lead + (CHUNK, BAND - own), NEG, jnp.float32)
    per_head = jnp.concatenate([jnp.concatenate([band, neg], axis=-1),
                                jnp.concatenate([neg, band], axis=-1)], axis=-2)
    return per_head.reshape(lead[0], N_PAIRS, 2 * Q_BLOCK, BAND)


def kernel(x, p, norm_mix_g, w_in, qn_g, kn_g, rel_bias, conv_w, conv_b, conv_ln_g,
           conv_ln_b, conv_pw_w, conv_pw_b, ret_gn_g, w_o, norm_ffn_g, w1, w2,
           norm_ple_g, w_pg, w_ple):
    depth = w_in.shape[0]
    b, s, _ = x.shape
    assert s % SEQ_TILE == 0 and (b * s) % FFN_TILE == 0
    bf16 = jnp.bfloat16
    row = lambda a: a[:, None, :]
    prm = dict(
        g_mix=row(norm_mix_g), w_in=w_in,
        qn_g=row(jnp.tile(qn_g, (1, 2))), kn_g=row(jnp.tile(kn_g, (1, 2))),
        bias=_attention_bias(rel_bias),
        conv_w=jnp.pad(conv_w, ((0, 0), (0, 1), (0, 0))), conv_b=row(conv_b),
        ln_g=row(conv_ln_g), ln_b=row(conv_ln_b),
        pw_w=conv_pw_w.astype(bf16), pw_b=row(conv_pw_b),
        gn_g=ret_gn_g.reshape(depth, N_PAIRS, 1, LANES), w_o=w_o,
        g_ffn=row(norm_ffn_g), w1=w1, w2=w2,
        g_ple=row(norm_ple_g), w_pg=w_pg, w_ple=w_ple,
        p=p.reshape(depth, b * s, D_PLE),
    )
    consts = _retention_consts(s)
    h = x
    for layer in range(depth):
        h = _mixer_layer(h, layer, prm, consts)
        h = _ffn_layer(h, layer, prm)
    return h
```

```python
import math
from functools import partial

import numpy as np
import jax
import jax.numpy as jnp
from jax import lax
from jax.experimental import pallas as pl
from jax.experimental.pallas import tpu as pltpu

D_MODEL = 1024
CHUNK = 64
N_LEFT_CHUNKS = 8
HEAD_DIM = 64
D_ATT = 384
D_RET = 384
D_CONV = 256
H_ATT = D_ATT // HEAD_DIM
H_RET = D_RET // HEAD_DIM
CONV_K = 31
REL_CLIP = 128
D_FF = 4 * D_MODEL
D_PLE = 256
ROPE_THETA = 10000.0
EPS = 1e-6
D_IN = 3 * D_ATT + 2 * D_CONV + 4 * D_RET

OFF_QA, OFF_KA, OFF_VA = 0, D_ATT, 2 * D_ATT
OFF_CA = 3 * D_ATT
OFF_CG = OFF_CA + D_CONV
OFF_QR = OFF_CA + 2 * D_CONV
OFF_KR = OFF_QR + D_RET
OFF_VR = OFF_KR + D_RET
OFF_GR = OFF_VR + D_RET
MIX_ATT, MIX_CONV, MIX_RET = 0, D_ATT, D_ATT + D_CONV

LANES = 128
SUBLANES = 8
N_PAIRS = D_ATT // LANES
SEQ_TILE = N_LEFT_CHUNKS * CHUNK
Q_BLOCK = 2 * CHUNK
BAND = (N_LEFT_CHUNKS + 2) * CHUNK
CONV_HIST = 32
CONV_ROWS = 64
NORM_ROWS = 64
ATT_PROJ_COLS = 384
LATE_PROJ_COLS = 256
NEXT_AT_ATT_BLOCKS = (2, 5, 8)
RET_AFTER_ATT_BLOCKS = 8
RET_STEPS_PER_ATT_BLOCK = 2
OUT_ROWS = SEQ_TILE
CONV_ROWS_IN_ATTENTION = 512
FFN_TILE = 512
FF_BLOCK = 512
NEG = -1e30
LOG2E = math.log2(math.e)
VMEM_LIMIT_BYTES = 56 * 1024 * 1024

_NT = (((1,), (1,)), ((), ()))
_TN = (((0,), (0,)), ((), ()))


def _rms(x, g):
    return (x * lax.rsqrt(jnp.mean(x * x, axis=-1, keepdims=True) + EPS)) * g


def _dot(a, b):
    return jnp.dot(a, b, preferred_element_type=jnp.float32)


def _pair_sum(x, low):
    s0 = jnp.sum(jnp.where(low, x, 0.0), axis=-1, keepdims=True)
    s1 = jnp.sum(jnp.where(low, 0.0, x), axis=-1, keepdims=True)
    return jnp.where(low, s0, s1)


def _pair_rms(x, low, g):
    ms = _pair_sum(x * x, low) * (1.0 / HEAD_DIM)
    return (x * lax.rsqrt(ms + EPS)) * g


def _swap_halves(x, first_half):
    return jnp.where(first_half, pltpu.roll(x, 96, 1), pltpu.roll(x, 32, 1))


def _mixer_kernel(hc_ref, hn_ref, g_ref, w_in_ref, qn_ref, kn_ref, bias_ref,
                  cw_ref, cb_ref, lng_ref, lnb_ref, pww_ref, pwb_ref,
                  cos_ref, sin_ref, dmat_ref, qdec_ref, kdec_ref, cdec_ref, gn_ref,
                  w_o_ref, o_ref, xn_s, xnn_s, pnext_s, proj, k_s, v_s, glu_s, st_s, mix_s, *,
                  tiles_per_seq):
    ts = SEQ_TILE
    step = pl.program_id(0)
    t = lax.rem(step, tiles_per_seq)

    def att_projection(c0):
        pnext_s[:, c0:c0 + ATT_PROJ_COLS] = _dot(
            xnn_s[...], w_in_ref[:, c0:c0 + ATT_PROJ_COLS].astype(jnp.bfloat16))

    @pl.when(step == 0)
    def _():
        xnn_s[...] = _rms(hc_ref[...], g_ref[...]).astype(jnp.bfloat16)
        for c0 in range(0, OFF_CA, ATT_PROJ_COLS):
            att_projection(c0)

    @pl.when(t == 0)
    def _():
        k_s[0:ts, :] = jnp.zeros((ts, D_ATT), jnp.bfloat16)
        v_s[0:ts, :] = jnp.zeros((ts, D_ATT), jnp.bfloat16)
        glu_s[0:CONV_HIST, :] = jnp.zeros((CONV_HIST, D_CONV), jnp.float32)
        st_s[...] = jnp.zeros_like(st_s)

    xn_s[...] = _rms(hc_ref[...], g_ref[...]).astype(jnp.bfloat16)
    pieces = list(range(OFF_CA, D_IN, LATE_PROJ_COLS))
    next_pieces = list(range(0, OFF_CA, ATT_PROJ_COLS))

    def late_projection(n):
        for _ in range(n):
            if pieces:
                c0 = pieces.pop(0)
                proj[:, c0:c0 + LATE_PROJ_COLS] = _dot(
                    xn_s[...], w_in_ref[:, c0:c0 + LATE_PROJ_COLS].astype(jnp.bfloat16))

    lane = lax.broadcasted_iota(jnp.int32, (1, LANES), 1)
    low = lane < HEAD_DIM
    first_half = (lane & 32) == 0

    def fill_glu():
        a = proj[:, OFF_CA:OFF_CA + D_CONV]
        gate = proj[:, OFF_CG:OFF_CG + D_CONV]
        glu_s[CONV_HIST:CONV_HIST + ts, :] = a * jax.nn.sigmoid(gate)

    base = CONV_HIST - (CONV_K - 1)

    def conv_block(r):
        acc = None
        for rho in range(SUBLANES):
            rows = CONV_ROWS + (SUBLANES if rho else 0)
            z = None
            for kk in range(CONV_K):
                if (base + kk) % SUBLANES != rho:
                    continue
                off = r + base + kk - rho
                term = cw_ref[kk:kk + 1, :] * glu_s[off:off + rows, :]
                z = term if z is None else z + term
            z = z[rho:rho + CONV_ROWS]
            acc = z if acc is None else acc + z
        y = acc + cb_ref[...]
        mu = jnp.mean(y, axis=-1, keepdims=True)
        yc = y - mu
        var = jnp.mean(yc * yc, axis=-1, keepdims=True)
        y = (yc * lax.rsqrt(var + EPS)) * lng_ref[...] + lnb_ref[...]
        y = y * jax.nn.sigmoid(y)
        y = _dot(y.astype(jnp.bfloat16), pww_ref[...]) + pwb_ref[...]
        mix_s[r:r + CONV_ROWS, MIX_CONV:MIX_CONV + D_CONV] = y.astype(jnp.bfloat16)

    row2 = lax.broadcasted_iota(jnp.int32, (2 * CHUNK, LANES), 0)
    lane2 = lax.broadcasted_iota(jnp.int32, (2 * CHUNK, LANES), 1)
    own2 = (row2 < CHUNK) == (lane2 < HEAD_DIM)

    def retention_step(n):
        rs = slice(n * CHUNK, (n + 1) * CHUNK)
        cos = cos_ref[rs, :]
        sin = sin_ref[rs, :]
        for p in range(N_PAIRS):
            q = proj[rs, OFF_QR + p * LANES:OFF_QR + (p + 1) * LANES]
            k = proj[rs, OFF_KR + p * LANES:OFF_KR + (p + 1) * LANES]
            v = proj[rs, OFF_VR + p * LANES:OFF_VR + (p + 1) * LANES]
            g = proj[rs, OFF_GR + p * LANES:OFF_GR + (p + 1) * LANES]
            qr = (q * cos + _swap_halves(q, first_half) * sin) * (HEAD_DIM ** -0.5)
            kr = k * cos + _swap_halves(k, first_half) * sin
            k2 = jnp.where(own2, jnp.concatenate([kr, kr], axis=0), 0.0).astype(jnp.bfloat16)
            v2 = jnp.where(own2, jnp.concatenate([v, v], axis=0), 0.0).astype(jnp.bfloat16)
            s = lax.dot_general(qr.astype(jnp.bfloat16), k2, _NT,
                                preferred_element_type=jnp.float32)
            sd = (s * dmat_ref[p]).astype(jnp.bfloat16)
            qd = (qr * qdec_ref[p]).astype(jnp.bfloat16)
            state = st_s[p]
            lhs = jnp.concatenate([qd, sd], axis=1)
            rhs = jnp.concatenate([state.astype(jnp.bfloat16), v2], axis=0)
            o = _dot(lhs, rhs)
            cen = o - _pair_sum(o, low) * (1.0 / HEAD_DIM)
            var = _pair_sum(cen * cen, low) * (1.0 / HEAD_DIM)
            y = (cen * lax.rsqrt(var + EPS)) * gn_ref[p]
            y = (g * jax.nn.sigmoid(g)) * y
            mix_s[rs, MIX_RET + p * LANES:MIX_RET + (p + 1) * LANES] = y.astype(jnp.bfloat16)
            kd = (kr * kdec_ref[p]).astype(jnp.bfloat16)
            kv = lax.dot_general(kd, v.astype(jnp.bfloat16), _TN,
                                 preferred_element_type=jnp.float32)
            st_s[p] = cdec_ref[p] * state + jnp.where(own2, kv, 0.0)

    for p in range(N_PAIRS):
        for r0 in range(0, ts, NORM_ROWS):
            rs = slice(r0, r0 + NORM_ROWS)
            qc = slice(OFF_QA + p * LANES, OFF_QA + (p + 1) * LANES)
            kc = slice(OFF_KA + p * LANES, OFF_KA + (p + 1) * LANES)
            proj[rs, qc] = (_pair_rms(pnext_s[rs, qc], low, qn_ref[...])
                            * (HEAD_DIM ** -0.5 * LOG2E))
            k_s[ts + r0:ts + r0 + NORM_ROWS, p * LANES:(p + 1) * LANES] = (
                _pair_rms(pnext_s[rs, kc], low, kn_ref[...]).astype(jnp.bfloat16))
    for r0 in range(0, ts, NORM_ROWS):
        v_s[ts + r0:ts + r0 + NORM_ROWS, :] = (
            pnext_s[r0:r0 + NORM_ROWS, OFF_VA:OFF_VA + D_ATT].astype(jnp.bfloat16))
    xnn_s[...] = _rms(hn_ref[...], g_ref[...]).astype(jnp.bfloat16)

    col = lax.broadcasted_iota(jnp.int32, (1, BAND), 1)
    n_att_blocks = 0
    glu_filled = False
    conv_rows_done = 0
    ret_steps_done = 0
    out_rows_done = 0

    def out_projection(r):
        o_ref[r:r + OUT_ROWS, :] = hc_ref[r:r + OUT_ROWS, :] + _dot(
            mix_s[r:r + OUT_ROWS, :], w_o_ref[...].astype(jnp.bfloat16))
    for c2 in range(ts // Q_BLOCK):
        r0 = c2 * Q_BLOCK
        first_valid = jnp.where(t == 0, ts - r0, 0)
        key_ok = col >= first_valid
        for p in range(N_PAIRS):
            sl = slice(p * LANES, (p + 1) * LANES)
            q = proj[r0:r0 + Q_BLOCK, OFF_QA + p * LANES:OFF_QA + (p + 1) * LANES]
            q2 = jnp.concatenate([jnp.where(low, q, 0.0), jnp.where(low, 0.0, q)],
                                 axis=0).astype(jnp.bfloat16)
            kb = k_s[r0:r0 + BAND, sl]
            vb = v_s[r0:r0 + BAND, sl]
            s = lax.dot_general(q2, kb, _NT, preferred_element_type=jnp.float32)
            s = jnp.where(key_ok, s + bias_ref[p], NEG)
            m = jnp.max(s, axis=-1, keepdims=True)
            e = jnp.exp2(s - m)
            den = jnp.sum(e, axis=-1, keepdims=True)
            o2 = _dot(e.astype(jnp.bfloat16), vb) / den
            o = jnp.where(low, o2[0:Q_BLOCK], o2[Q_BLOCK:2 * Q_BLOCK])
            mix_s[r0:r0 + Q_BLOCK, MIX_ATT + p * LANES:MIX_ATT + (p + 1) * LANES] = (
                o.astype(jnp.bfloat16))
            n_att_blocks += 1
            late_projection(1)
            conv_cols_ready = not pieces or pieces[0] >= OFF_QR
            if conv_cols_ready and not glu_filled:
                fill_glu()
                glu_filled = True
            elif glu_filled and conv_rows_done < CONV_ROWS_IN_ATTENTION:
                conv_block(conv_rows_done)
                conv_rows_done += CONV_ROWS
            if n_att_blocks in NEXT_AT_ATT_BLOCKS and next_pieces:
                att_projection(next_pieces.pop(0))
            if not pieces:
                for _ in range(RET_STEPS_PER_ATT_BLOCK):
                    if n_att_blocks > RET_AFTER_ATT_BLOCKS and ret_steps_done < ts // CHUNK:
                        retention_step(ret_steps_done)
                        ret_steps_done += 1
            att_rows_done = r0 + (Q_BLOCK if p == N_PAIRS - 1 else 0)
            ready = min(att_rows_done, conv_rows_done, ret_steps_done * CHUNK)
            while out_rows_done + OUT_ROWS <= ready:
                out_projection(out_rows_done)
                out_rows_done += OUT_ROWS
    late_projection(len(pieces))
    while next_pieces:
        att_projection(next_pieces.pop(0))
    k_s[0:ts, :] = k_s[ts:2 * ts, :]
    v_s[0:ts, :] = v_s[ts:2 * ts, :]

    if not glu_filled:
        fill_glu()
    while conv_rows_done < ts:
        conv_block(conv_rows_done)
        conv_rows_done += CONV_ROWS
    while ret_steps_done < ts // CHUNK:
        retention_step(ret_steps_done)
        ret_steps_done += 1
    glu_s[0:CONV_HIST, :] = glu_s[ts:ts + CONV_HIST, :]
    while out_rows_done < ts:
        out_projection(out_rows_done)
        out_rows_done += OUT_ROWS


def _ffn_kernel(h_ref, p_ref, gf_ref, w1_ref, w2_ref, gp_ref, wpg_ref, wple_ref, o_ref):
    x = h_ref[...]
    hn = _rms(x, gf_ref[...]).astype(jnp.bfloat16)
    acc = jnp.zeros_like(x)
    for j in range(0, D_FF, FF_BLOCK):
        u = jnp.maximum(_dot(hn, w1_ref[:, j:j + FF_BLOCK].astype(jnp.bfloat16)), 0.0)
        acc = acc + _dot((u * u).astype(jnp.bfloat16),
                         w2_ref[j:j + FF_BLOCK, :].astype(jnp.bfloat16))
    h2 = x + acc
    gn = _rms(h2, gp_ref[...]).astype(jnp.bfloat16)
    gate = jax.nn.sigmoid(_dot(gn, wpg_ref[...].astype(jnp.bfloat16)))
    ple = _dot(p_ref[...].astype(jnp.bfloat16), wple_ref[...].astype(jnp.bfloat16))
    o_ref[...] = h2 + gate * ple


def _const_spec(shape):
    nd = len(shape)
    return pl.BlockSpec(shape, lambda *_: (0,) * nd, pipeline_mode=pl.Buffered(1))


def _layer_spec(shape, layer):
    nd = len(shape)
    return pl.BlockSpec((None,) + tuple(shape[1:]), lambda *_: (layer,) + (0,) * (nd - 1),
                        pipeline_mode=pl.Buffered(1))


def _mixer_layer(h, layer, prm, consts):
    b, s, _ = h.shape
    ts = SEQ_TILE
    nt = s // ts
    n_steps = b * nt
    h3 = h.reshape(n_steps, ts, D_MODEL)
    stacked = [prm[k] for k in ("g_mix", "w_in", "qn_g", "kn_g", "bias", "conv_w", "conv_b",
                                "ln_g", "ln_b", "pw_w", "pw_b")]
    tables = [consts[k] for k in ("dmat", "qdec", "kdec", "cdec")]
    cur_spec = pl.BlockSpec((None, ts, D_MODEL), lambda i: (i, 0, 0))
    next_spec = pl.BlockSpec((None, ts, D_MODEL),
                             lambda i: (jnp.minimum(i + 1, n_steps - 1), 0, 0))
    rope_spec = pl.BlockSpec((ts, LANES), lambda i: (lax.rem(i, nt), 0))
    in_specs = ([cur_spec, next_spec] + [_layer_spec(a.shape, layer) for a in stacked]
                + [rope_spec, rope_spec] + [_const_spec(a.shape) for a in tables]
                + [_layer_spec(prm["gn_g"].shape, layer), _layer_spec(prm["w_o"].shape, layer)])
    out = pl.pallas_call(
        partial(_mixer_kernel, tiles_per_seq=nt),
        grid=(n_steps,),
        in_specs=in_specs,
        out_specs=cur_spec,
        out_shape=jax.ShapeDtypeStruct(h3.shape, jnp.float32),
        scratch_shapes=[
            pltpu.VMEM((ts, D_MODEL), jnp.bfloat16),
            pltpu.VMEM((ts, D_MODEL), jnp.bfloat16),
            pltpu.VMEM((ts, OFF_CA), jnp.float32),
            pltpu.VMEM((ts, D_IN), jnp.float32),
            pltpu.VMEM((2 * ts, D_ATT), jnp.bfloat16),
            pltpu.VMEM((2 * ts, D_ATT), jnp.bfloat16),
            pltpu.VMEM((CONV_HIST + ts, D_CONV), jnp.float32),
            pltpu.VMEM((N_PAIRS, LANES, LANES), jnp.float32),
            pltpu.VMEM((ts, D_MODEL), jnp.bfloat16),
        ],
        compiler_params=pltpu.CompilerParams(
            dimension_semantics=("arbitrary",),
            vmem_limit_bytes=VMEM_LIMIT_BYTES),
        name="mixer_layer",
    )(h3, h3, *stacked, consts["cos"], consts["sin"], *tables, prm["gn_g"], prm["w_o"])
    return out.reshape(b, s, D_MODEL)


def _ffn_layer(h, layer, prm):
    b, s, _ = h.shape
    t = b * s
    stacked = [prm[k] for k in ("g_ffn", "w1", "w2", "g_ple", "w_pg", "w_ple")]
    in_specs = ([pl.BlockSpec((FFN_TILE, D_MODEL), lambda i: (i, 0)),
                 pl.BlockSpec((None, FFN_TILE, D_PLE), lambda i: (layer, i, 0))]
                + [_layer_spec(a.shape, layer) for a in stacked])
    out = pl.pallas_call(
        _ffn_kernel,
        grid=(t // FFN_TILE,),
        in_specs=in_specs,
        out_specs=pl.BlockSpec((FFN_TILE, D_MODEL), lambda i: (i, 0)),
        out_shape=jax.ShapeDtypeStruct((t, D_MODEL), jnp.float32),
        compiler_params=pltpu.CompilerParams(
            dimension_semantics=("arbitrary",),
            vmem_limit_bytes=VMEM_LIMIT_BYTES),
        name="ffn_ple_layer",
    )(h.reshape(t, D_MODEL), prm["p"], *stacked)
    return out.reshape(b, s, D_MODEL)


def _pair_lanes(per_head):
    x = np.repeat(per_head[..., None], HEAD_DIM, axis=-1)
    x = x.reshape((N_PAIRS, 2) + x.shape[1:])
    return np.concatenate([x[:, 0], x[:, 1]], axis=-1)


def _retention_consts(s):
    pos = np.arange(s, dtype=np.float64)
    inv_freq = ROPE_THETA ** (-np.arange(0, HEAD_DIM, 2, dtype=np.float64) / HEAD_DIM)
    ang = pos[:, None] * inv_freq[None, :]
    cos, sin = np.cos(ang), np.sin(ang)
    cos4 = np.concatenate([cos, cos, cos, cos], axis=-1)
    sin4 = np.concatenate([-sin, sin, -sin, sin], axis=-1)
    log_gamma = np.log(1.0 - 2.0 ** (-5.0 - np.arange(H_RET, dtype=np.float64)))
    n = np.arange(CHUNK, dtype=np.float64)
    dist = np.abs(n[:, None] - n[None, :])
    d_intra = np.exp(log_gamma[:, None, None] * dist[None])
    d_pair = d_intra.reshape(N_PAIRS, 2, CHUNK, CHUNK)
    dmat = np.concatenate([d_pair[:, 0], d_pair[:, 1]], axis=-1)
    q_dec = np.exp(log_gamma[:, None] * (n + 1.0)[None, :])
    k_dec = np.exp(log_gamma[:, None] * (CHUNK - 1 - n)[None, :])
    chunk_decay = np.exp(log_gamma * CHUNK).reshape(N_PAIRS, 2)
    qdec = _pair_lanes(q_dec)
    kdec = _pair_lanes(k_dec)
    ones = np.ones((HEAD_DIM, HEAD_DIM))
    cdec = np.stack([np.kron(np.diag(cd), ones) for cd in chunk_decay])
    tabs = dict(cos=cos4, sin=sin4, dmat=dmat, qdec=qdec, kdec=kdec, cdec=cdec)
    return {k: jnp.asarray(np.ascontiguousarray(v), dtype=jnp.float32) for k, v in tabs.items()}


def _attention_bias(rel_bias):
    lead = rel_bias.shape[:-1]
    own = (N_LEFT_CHUNKS + 1) * CHUNK
    n_m = own + CHUNK - 1
    n_far = N_LEFT_CHUNKS * CHUNK + CHUNK - REL_CLIP
    n_near = n_m - n_far
    rb = rel_bias.astype(jnp.float32) * LOG2E
    far = jnp.broadcast_to(rb[..., 2 * REL_CLIP:], lead + (n_far,))
    near = rb[..., 2 * REL_CLIP - n_near:2 * REL_CLIP][..., ::-1]
    vec = jnp.concatenate([far, near, jnp.zeros(lead + (1,), jnp.float32)], axis=-1)
    skew = jnp.tile(vec, CHUNK)[..., :CHUNK * n_m].reshape(lead + (CHUNK, n_m))
    band = skew[..., CHUNK - 1:CHUNK - 1 + own]
    neg = jnp.full(lead + (CHUNK, BAND - own), NEG, jnp.float32)
    per_head = jnp.concatenate([jnp.concatenate([band, neg], axis=-1),
                                jnp.concatenate([neg, band], axis=-1)], axis=-2)
    return per_head.reshape(lead[0], N_PAIRS, 2 * Q_BLOCK, BAND)


def kernel(x, p, norm_mix_g, w_in, qn_g, kn_g, rel_bias, conv_w, conv_b, conv_ln_g,
           conv_ln_b, conv_pw_w, conv_pw_b, ret_gn_g, w_o, norm_ffn_g, w1, w2,
           norm_ple_g, w_pg, w_ple):
    depth = w_in.shape[0]
    b, s, _ = x.shape
    assert s % SEQ_TILE == 0 and (b * s) % FFN_TILE == 0
    bf16 = jnp.bfloat16
    row = lambda a: a[:, None, :]
    prm = dict(
        g_mix=row(norm_mix_g), w_in=w_in,
        qn_g=row(jnp.tile(qn_g, (1, 2))), kn_g=row(jnp.tile(kn_g, (1, 2))),
        bias=_attention_bias(rel_bias),
        conv_w=jnp.pad(conv_w, ((0, 0), (0, 1), (0, 0))), conv_b=row(conv_b),
        ln_g=row(conv_ln_g), ln_b=row(conv_ln_b),
        pw_w=conv_pw_w.astype(bf16), pw_b=row(conv_pw_b),
        gn_g=ret_gn_g.reshape(depth, N_PAIRS, 1, LANES), w_o=w_o,
        g_ffn=row(norm_ffn_g), w1=w1, w2=w2,
        g_ple=row(norm_ple_g), w_pg=w_pg, w_ple=w_ple,
        p=p.reshape(depth, b * s, D_PLE),
    )
    consts = _retention_consts(s)
    h = x
    for layer in range(depth):
        h = _mixer_layer(h, layer, prm, consts)
        h = _ffn_layer(h, layer, prm)
    return h
```
